```python
import math
import jax, jax.numpy as jnp
from jax import lax
import numpy as np

D_MODEL = 1024
BATCH = 16
SEQ = 2048
DEPTH = 4

HEAD_DIM = 64
ROPE_DIM = HEAD_DIM // 4
ROPE_THETA = 500000.0
Q_BLOCK = 128
LN_EPS = 1e-5
DIFF_HEADS = 4
NSA_HEADS = 8
NSA_KV_HEADS = 2
NSA_CMP_LEN = 32
NSA_CMP_STRIDE = 16
NSA_CMP_HIDDEN = 256
NSA_SEL_LEN = 64
NSA_TOPK = 8
NSA_WINDOW = 512
NSA_Q_BLOCK = 64
NSA_N_BRANCH = 3
NSA_BIG = 1e4
NEG = -1e30
FOX_HEADS = D_MODEL // HEAD_DIM
FOX_WIDTH = FOX_HEADS * HEAD_DIM
D_FF = 4 * D_MODEL
ALPHA = (2 * DEPTH) ** 0.25
BETA = (8 * DEPTH) ** -0.25
N_EVEN = (DEPTH + 1) // 2
N_ODD = DEPTH // 2

DIFF_QK = DIFF_HEADS * 2 * HEAD_DIM
DIFF_V = DIFF_HEADS * 2 * HEAD_DIM
NSA_Q = NSA_HEADS * HEAD_DIM
NSA_KV = NSA_KV_HEADS * HEAD_DIM
EVEN_SPLITS = (DIFF_QK, DIFF_QK, DIFF_V, NSA_Q, NSA_KV, NSA_KV, NSA_KV, NSA_KV, NSA_KV, NSA_KV, NSA_HEADS * NSA_N_BRANCH)
EVEN_COLS = sum(EVEN_SPLITS)
EVEN_MIX = DIFF_V + NSA_Q
ODD_SPLITS = (FOX_WIDTH, FOX_WIDTH, FOX_WIDTH, FOX_HEADS)
ODD_COLS = sum(ODD_SPLITS)

kernel_name = "hybrid_diff_nsa_fox_deepnorm"


def _split(h, sizes):
    offsets = [int(o) for o in np.cumsum(sizes)[:-1]]
    return jnp.split(h, offsets, axis=-1)


def layer_norm(x, g, b):
    xf = x.astype(jnp.float32)
    mu = jnp.mean(xf, axis=-1, keepdims=True)
    var = jnp.mean(jnp.square(xf - mu), axis=-1, keepdims=True)
    return ((xf - mu) * lax.rsqrt(var + LN_EPS) * g.astype(jnp.float32) + b.astype(jnp.float32)).astype(x.dtype)


def rms_norm(x, g):
    xf = x.astype(jnp.float32)
    return xf * lax.rsqrt(jnp.mean(jnp.square(xf), axis=-1, keepdims=True) + LN_EPS) * g.astype(jnp.float32)


def partial_rope(x, pos):
    half = ROPE_DIM // 2
    inv = ROPE_THETA ** (-jnp.arange(half, dtype=jnp.float32) / half)
    ang = pos.astype(jnp.float32)[:, None] * inv[None, :]
    shape = (pos.shape[0],) + (1,) * (x.ndim - 3) + (half,)
    cos = jnp.cos(ang).reshape(shape)
    sin = jnp.sin(ang).reshape(shape)
    xf = x.astype(jnp.float32)
    x1 = xf[..., :half]
    x2 = xf[..., half:ROPE_DIM]
    return jnp.concatenate([x1 * cos - x2 * sin, x2 * cos + x1 * sin, xf[..., ROPE_DIM:]], axis=-1)


def _sweep(fn, seq_len, block):
    out = lax.map(fn, jnp.arange(seq_len // block))
    out = jnp.moveaxis(out, 0, 1)
    return out.reshape((out.shape[0], seq_len) + out.shape[3:])


def diff_attention(q, k, v, lam_params, subln_g, layer_idx):
    B, S = q.shape[:2]
    pos = jnp.arange(S)
    scale = HEAD_DIM ** -0.5
    qr = partial_rope(q, pos) * scale
    kr = partial_rope(k, pos)
    vf = v.astype(jnp.float32)
    lam_init = 0.8 - 0.6 * math.exp(-0.3 * layer_idx)
    lp = lam_params.astype(jnp.float32)
    lam = jnp.exp(jnp.sum(lp[0] * lp[1])) - jnp.exp(jnp.sum(lp[2] * lp[3])) + lam_init

    def blk(i):
        q0 = i * Q_BLOCK
        qb = lax.dynamic_slice_in_dim(qr, q0, Q_BLOCK, axis=1)
        tq = q0 + jnp.arange(Q_BLOCK)
        mask = pos[None, :] <= tq[:, None]
        s = jnp.einsum('bqhcd,bshcd->bhcqs', qb, kr)
        p = jax.nn.softmax(jnp.where(mask, s, -jnp.inf), axis=-1)
        pd = p[:, :, 0] - lam * p[:, :, 1]
        return jnp.einsum('bhqs,bshe->bqhe', pd, vf)

    o = _sweep(blk, S, Q_BLOCK)
    o = rms_norm(o, subln_g) * (1.0 - lam_init)
    return o.reshape(B, S, -1).astype(v.dtype)


def nsa_attention(q, k_cmp, v_cmp, k_slc, v_slc, k_win, v_win, gate_logits, pe, w1, b1, w2):
    B, S, H, d = q.shape
    G = NSA_KV_HEADS
    HPG = H // G
    W = NSA_WINDOW
    pos = jnp.arange(S)
    scale = HEAD_DIM ** -0.5
    qr = (partial_rope(q, pos) * scale).reshape(B, S, G, HPG, d)

    n_cmp = (S - NSA_CMP_LEN) // NSA_CMP_STRIDE + 1
    cmp_start = jnp.arange(n_cmp) * NSA_CMP_STRIDE
    cmp_end = cmp_start + NSA_CMP_LEN - 1
    cmp_idx = cmp_start[:, None] + jnp.arange(NSA_CMP_LEN)[None, :]

    def compress(t, j):
        blocks = t[:, cmp_idx] + pe[j][None, None, :, None, :]
        blocks = jnp.moveaxis(blocks, 3, 2).reshape(B, n_cmp, G, NSA_CMP_LEN * d)
        h = jax.nn.gelu(blocks @ w1[j] + b1[j])
        return h @ w2[j]

    kc = partial_rope(compress(k_cmp, 0), cmp_end)
    vc = compress(v_cmp, 1).astype(jnp.float32)

    n_sel = S // NSA_SEL_LEN
    topk = min(NSA_TOPK, n_sel)
    ks = partial_rope(k_slc, pos)
    ks_blk = ks.reshape(B, n_sel, NSA_SEL_LEN, G, d).transpose(0, 3, 1, 2, 4)
    vs_blk = v_slc.astype(jnp.float32).reshape(B, n_sel, NSA_SEL_LEN, G, d).transpose(0, 3, 1, 2, 4)
    sel_start = jnp.arange(n_sel) * NSA_SEL_LEN
    overlap = ((cmp_start[:, None] <= sel_start[None, :] + NSA_SEL_LEN - 1)
               & (cmp_end[:, None] >= sel_start[None, :])).astype(jnp.float32)
    bi = jnp.arange(B)[:, None, None, None]
    gi = jnp.arange(G)[None, :, None, None]
    jb = jnp.arange(n_sel)

    kw_pad = jnp.pad(partial_rope(k_win, pos), ((0, 0), (W, 0), (0, 0), (0, 0)))
    vw_pad = jnp.pad(v_win.astype(jnp.float32), ((0, 0), (W, 0), (0, 0), (0, 0)))

    gate = jax.nn.sigmoid(gate_logits.astype(jnp.float32)).reshape(B, S, G, HPG, NSA_N_BRANCH)

    def blk(i):
        q0 = i * NSA_Q_BLOCK
        qb = lax.dynamic_slice_in_dim(qr, q0, NSA_Q_BLOCK, axis=1)
        tq = q0 + jnp.arange(NSA_Q_BLOCK)
        valid_c = cmp_end[None, :] <= tq[:, None]
        sc = jnp.einsum('bqghd,bngd->bghqn', qb, kc)
        pc = jax.nn.softmax(jnp.where(valid_c, sc, NEG), axis=-1) * valid_c
        o_c = jnp.einsum('bghqn,bngd->bqghd', pc, vc)
        imp = jnp.einsum('bghqn,nj->bgqj', pc, overlap)
        cur = tq // NSA_SEL_LEN
        forced = (jb[None, :] == 0) | (jb[None, :] == cur[:, None]) | (jb[None, :] == cur[:, None] - 1)
        future = jb[None, :] > cur[:, None]
        imp = jnp.where(forced, NSA_BIG, jnp.where(future, -NSA_BIG, imp))
        _, idx = lax.top_k(imp, topk)
        kg = ks_blk[bi, gi, idx]
        vg = vs_blk[bi, gi, idx]
        tok = idx[..., None] * NSA_SEL_LEN + jnp.arange(NSA_SEL_LEN)
        valid_s = (tok <= tq[None, None, :, None, None])[:, :, None]
        ss = jnp.where(valid_s, jnp.einsum('bqghd,bgqkld->bghqkl', qb, kg), -jnp.inf)
        ps = jax.nn.softmax(ss.reshape(ss.shape[:4] + (-1,)), axis=-1).reshape(ss.shape)
        o_s = jnp.einsum('bghqkl,bgqkld->bqghd', ps, vg)
        kwb = lax.dynamic_slice_in_dim(kw_pad, q0, W + NSA_Q_BLOCK, axis=1)
        vwb = lax.dynamic_slice_in_dim(vw_pad, q0, W + NSA_Q_BLOCK, axis=1)
        sk = q0 - W + jnp.arange(W + NSA_Q_BLOCK)
        valid_w = (sk[None, :] <= tq[:, None]) & (sk[None, :] > tq[:, None] - W) & (sk[None, :] >= 0)
        sw = jnp.einsum('bqghd,bsgd->bghqs', qb, kwb)
        pw = jax.nn.softmax(jnp.where(valid_w, sw, -jnp.inf), axis=-1)
        o_w = jnp.einsum('bghqs,bsgd->bqghd', pw, vwb)
        gb = lax.dynamic_slice_in_dim(gate, q0, NSA_Q_BLOCK, axis=1)
        return gb[..., 0:1] * o_c + gb[..., 1:2] * o_s + gb[..., 2:3] * o_w

    o = _sweep(blk, S, NSA_Q_BLOCK)
    return o.reshape(B, S, H * d).astype(q.dtype)


def forgetting_attention(q, k, v, f_logit, f_bias):
    B, S, H, d = q.shape
    pos = jnp.arange(S)
    qf = q.astype(jnp.float32) * (HEAD_DIM ** -0.5)
    kf = k.astype(jnp.float32)
    vf = v.astype(jnp.float32)
    log_f = jax.nn.log_sigmoid(f_logit.astype(jnp.float32) + f_bias.astype(jnp.float32))
    c = jnp.cumsum(log_f, axis=1).transpose(0, 2, 1)

    def blk(i):
        q0 = i * Q_BLOCK
        qb = lax.dynamic_slice_in_dim(qf, q0, Q_BLOCK, axis=1)
        cq = lax.dynamic_slice_in_dim(c, q0, Q_BLOCK, axis=2)
        tq = q0 + jnp.arange(Q_BLOCK)
        mask = pos[None, :] <= tq[:, None]
        s = jnp.einsum('bqhd,bshd->bhqs', qb, kf) + cq[..., None] - c[:, :, None, :]
        p = jax.nn.softmax(jnp.where(mask, s, -jnp.inf), axis=-1)
        return jnp.einsum('bhqs,bshd->bqhd', p, vf)

    o = _sweep(blk, S, Q_BLOCK)
    return o.reshape(B, S, H * d).astype(q.dtype)


def sqrelu_mlp(x, w_up, w_down):
    return jnp.square(jax.nn.relu(x @ w_up)) @ w_down


def setup_inputs(seed: int = 0) -> dict:
    key = jax.random.key(seed)
    ks = jax.random.split(key, 16)

    def nrm(k, shape, scale):
        return jax.random.normal(k, shape, jnp.float32) * scale

    return {
        'x': nrm(ks[0], (BATCH, SEQ, D_MODEL), 1.0),
        'ln_gain': 1.0 + nrm(ks[1], (DEPTH, 2, D_MODEL), 0.02),
        'ln_bias': nrm(ks[2], (DEPTH, 2, D_MODEL), 0.02),
        'mlp_w_up': nrm(ks[3], (DEPTH, D_MODEL, D_FF), D_MODEL ** -0.5),
        'mlp_w_down': nrm(ks[4], (DEPTH, D_FF, D_MODEL), BETA * D_FF ** -0.5),
        'w_in_even': nrm(ks[5], (N_EVEN, D_MODEL, EVEN_COLS), D_MODEL ** -0.5),
        'w_out_even': nrm(ks[6], (N_EVEN, EVEN_MIX, D_MODEL), BETA * EVEN_MIX ** -0.5),
        'diff_lambda': nrm(ks[7], (N_EVEN, 4, HEAD_DIM), 0.1),
        'diff_subln': 1.0 + nrm(ks[8], (N_EVEN, 2 * HEAD_DIM), 0.02),
        'nsa_pe': nrm(ks[9], (N_EVEN, 2, NSA_CMP_LEN, HEAD_DIM), 0.1),
        'nsa_cmp_w1': nrm(ks[10], (N_EVEN, 2, NSA_CMP_LEN * HEAD_DIM, NSA_CMP_HIDDEN), (NSA_CMP_LEN * HEAD_DIM) ** -0.5),
        'nsa_cmp_b1': nrm(ks[11], (N_EVEN, 2, NSA_CMP_HIDDEN), 0.02),
        'nsa_cmp_w2': nrm(ks[12], (N_EVEN, 2, NSA_CMP_HIDDEN, HEAD_DIM), NSA_CMP_HIDDEN ** -0.5),
        'w_in_odd': nrm(ks[13], (N_ODD, D_MODEL, ODD_COLS), D_MODEL ** -0.5),
        'fox_f_bias': jnp.linspace(1.0, 6.0, FOX_HEADS, dtype=jnp.float32)[None, :] + nrm(ks[14], (N_ODD, FOX_HEADS), 0.1),
        'w_out_odd': nrm(ks[15], (N_ODD, FOX_WIDTH, D_MODEL), BETA * FOX_WIDTH ** -0.5),
    }


def reference(x, ln_gain, ln_bias, mlp_w_up, mlp_w_down, w_in_even, w_out_even, diff_lambda, diff_subln,
              nsa_pe, nsa_cmp_w1, nsa_cmp_b1, nsa_cmp_w2, w_in_odd, fox_f_bias, w_out_odd):
    B, S, _ = x.shape
    for layer in range(DEPTH):
        li = layer // 2
        if layer % 2 == 0:
            h = x @ w_in_even[li]
            qa, ka, va, qn, kc, vc, ksl, vsl, kw, vw, g = _split(h, EVEN_SPLITS)
            o_a = diff_attention(qa.reshape(B, S, DIFF_HEADS, 2, HEAD_DIM),
                                 ka.reshape(B, S, DIFF_HEADS, 2, HEAD_DIM),
                                 va.reshape(B, S, DIFF_HEADS, 2 * HEAD_DIM),
                                 diff_lambda[li], diff_subln[li], layer)
            kv = lambda t: t.reshape(B, S, NSA_KV_HEADS, HEAD_DIM)
            o_b = nsa_attention(qn.reshape(B, S, NSA_HEADS, HEAD_DIM), kv(kc), kv(vc), kv(ksl), kv(vsl),
                                kv(kw), kv(vw), g, nsa_pe[li], nsa_cmp_w1[li], nsa_cmp_b1[li], nsa_cmp_w2[li])
            mix = jnp.concatenate([o_a, o_b], axis=-1) @ w_out_even[li]
        else:
            h = x @ w_in_odd[li]
            qf, kf, vf, fl = _split(h, ODD_SPLITS)
            hd = lambda t: t.reshape(B, S, FOX_HEADS, HEAD_DIM)
            mix = forgetting_attention(hd(qf), hd(kf), hd(vf), fl, fox_f_bias[li]) @ w_out_odd[li]
        x = layer_norm(ALPHA * x + mix, ln_gain[layer, 0], ln_bias[layer, 0])
        x = layer_norm(ALPHA * x + sqrelu_mlp(x, mlp_w_up[layer], mlp_w_down[layer]), ln_gain[layer, 1], ln_bias[layer, 1])
    return x
```

```python
import functools
import math

import numpy as np
import jax
import jax.numpy as jnp
from jax import lax
from jax.experimental import pallas as pl
from jax.experimental.pallas import tpu as pltpu

F32 = jnp.float32
BF16 = jnp.bfloat16

D_MODEL = 1024
DEPTH = 4
HEAD_DIM = 64
ROPE_DIM = HEAD_DIM // 4
ROPE_THETA = 500000.0
LN_EPS = 1e-5
DIFF_HEADS = 4
NSA_HEADS = 8
NSA_KV_HEADS = 2
NSA_HPG = NSA_HEADS // NSA_KV_HEADS
NSA_CMP_LEN = 32
NSA_CMP_STRIDE = 16
NSA_CMP_HIDDEN = 256
NSA_SEL_LEN = 64
NSA_TOPK = 8
NSA_WINDOW = 512
NSA_N_BRANCH = 3
NSA_BIG = 1e4
FOX_HEADS = D_MODEL // HEAD_DIM
D_FF = 4 * D_MODEL
ALPHA = (2 * DEPTH) ** 0.25

LANES = 128
NEG = -1e30
VMEM_LIMIT = 56 * 1024 * 1024

EVEN_ROPE_W = 512 + 512 + 512 + 128 + 128
EVEN_V_W = 512 + 128 + 128
EVEN_SEGS = ((0, EVEN_ROPE_W, True), (EVEN_ROPE_W, EVEN_V_W, False),
             (EVEN_ROPE_W + EVEN_V_W, 128, False), (EVEN_ROPE_W + EVEN_V_W + 128, 128, False),
             (EVEN_ROPE_W + EVEN_V_W + 256, 128, False))
EVEN_DTYPES = (BF16, BF16, BF16, BF16, F32)
ODD_SEGS = ((0, 3 * D_MODEL, False), (3 * D_MODEL, 128, False))
ODD_DTYPES = (BF16, F32)


def _dot(a, b):
    return jnp.dot(a, b, preferred_element_type=F32)


def _dot_nt(a, b):
    return lax.dot_general(a, b, (((1,), (1,)), ((), ())), preferred_element_type=F32)


def _rep_lanes(x, n):
    return x if n == 1 else jnp.concatenate([x] * n, axis=1)


def _rope(h, cos, sa, sb):
    return h * cos + pltpu.roll(h, LANES - ROPE_DIM // 2, 1) * sa + pltpu.roll(h, ROPE_DIM // 2, 1) * sb


def _layer_norm(y, g, b):
    mu = jnp.mean(y, axis=-1, keepdims=True)
    yc = y - mu
    var = jnp.mean(yc * yc, axis=-1, keepdims=True)
    return yc * lax.rsqrt(var + LN_EPS) * g + b


def _params(*sem):
    return pltpu.CompilerParams(dimension_semantics=sem, vmem_limit_bytes=VMEM_LIMIT)


def _proj_kernel(*refs, segs, has_rope):
    x_ref, w_ref = refs[0], refs[1]
    pos = 2
    if has_rope:
        cos_ref, sa_ref, sb_ref = refs[2:5]
        pos = 5
    out_refs = refs[pos:]
    xb = x_ref[...].astype(BF16)
    for o_ref, (c0, width, rope) in zip(out_refs, segs):
        for cc in range(0, width, 512):
            w = min(512, width - cc)
            h = _dot(xb, w_ref[:, c0 + cc:c0 + cc + w])
            if rope:
                cos, sa, sb = cos_ref[...], sa_ref[...], sb_ref[...]
                for t in range(0, w, LANES):
                    o_ref[:, cc + t:cc + t + LANES] = _rope(h[:, t:t + LANES], cos, sa, sb).astype(o_ref.dtype)
            else:
                o_ref[:, cc:cc + w] = h.astype(o_ref.dtype)


def _proj(x2d, w, tables, segs, dtypes, seq, tm=512):
    n, d = x2d.shape
    m = w.shape[1]
    ns = seq // tm
    has_rope = tables is not None
    in_specs = [pl.BlockSpec((tm, d), lambda i: (i, 0)), pl.BlockSpec((d, m), lambda i: (0, 0))]
    args = [x2d, w]
    if has_rope:
        in_specs += [pl.BlockSpec((tm, LANES), lambda i: (i % ns, 0))] * 3
        args += list(tables)
    return pl.pallas_call(
        functools.partial(_proj_kernel, segs=segs, has_rope=has_rope),
        grid=(n // tm,),
        in_specs=in_specs,
        out_specs=[pl.BlockSpec((tm, s[1]), lambda i: (i, 0)) for s in segs],
        out_shape=[jax.ShapeDtypeStruct((n, s[1]), dt) for s, dt in zip(segs, dtypes)],
        compiler_params=_params("parallel"),
        name="in_proj",
    )(*args)


def _outproj_ln_kernel(*refs, n_in):
    o_refs = refs[:n_in]
    w_refs = refs[n_in:2 * n_in]
    x_ref, g_ref, b_ref, y_ref = refs[2 * n_in:]
    mix = _dot(o_refs[0][...], w_refs[0][...])
    for o_ref, w_ref in zip(o_refs[1:], w_refs[1:]):
        mix = mix + _dot(o_ref[...], w_ref[...])
    y_ref[...] = _layer_norm(ALPHA * x_ref[...] + mix, g_ref[...], b_ref[...])


def _outproj_ln(os_, ws, x2d, g, b, tm=512):
    n, d = x2d.shape
    in_specs = [pl.BlockSpec((tm, o.shape[1]), lambda i: (i, 0)) for o in os_]
    in_specs += [pl.BlockSpec(w.shape, lambda i: (0, 0)) for w in ws]
    in_specs += [pl.BlockSpec((tm, d), lambda i: (i, 0)),
                 pl.BlockSpec((1, d), lambda i: (0, 0)), pl.BlockSpec((1, d), lambda i: (0, 0))]
    return pl.pallas_call(
        functools.partial(_outproj_ln_kernel, n_in=len(os_)),
        grid=(n // tm,),
        in_specs=in_specs,
        out_specs=pl.BlockSpec((tm, d), lambda i: (i, 0)),
        out_shape=jax.ShapeDtypeStruct((n, d), F32),
        compiler_params=_params("parallel"),
        name="out_proj_ln",
    )(*os_, *ws, x2d, g, b)


def _mlp_ln_kernel(x_ref, wu_ref, wd_ref, g_ref, b_ref, y_ref, *, ff_chunk):
    x = x_ref[...]
    xb = x.astype(BF16)
    acc = None
    for c0 in range(0, wu_ref.shape[1], ff_chunk):
        h = jnp.maximum(_dot(xb, wu_ref[:, c0:c0 + ff_chunk]), 0.0)
        part = _dot((h * h).astype(BF16), wd_ref[c0:c0 + ff_chunk, :])
        acc = part if acc is None else acc + part
    y_ref[...] = _layer_norm(ALPHA * x + acc, g_ref[...], b_ref[...])


def _mlp_ln(x2d, wu, wd, g, b, tm=512, ff_chunk=1024):
    n, d = x2d.shape
    resident = pl.Buffered(1)
    return pl.pallas_call(
        functools.partial(_mlp_ln_kernel, ff_chunk=ff_chunk),
        grid=(n // tm,),
        in_specs=[pl.BlockSpec((tm, d), lambda i: (i, 0)),
                  pl.BlockSpec(wu.shape, lambda i: (0, 0), pipeline_mode=resident),
                  pl.BlockSpec(wd.shape, lambda i: (0, 0), pipeline_mode=resident),
                  pl.BlockSpec((1, d), lambda i: (0, 0)), pl.BlockSpec((1, d), lambda i: (0, 0))],
        out_specs=pl.BlockSpec((tm, d), lambda i: (i, 0)),
        out_shape=jax.ShapeDtypeStruct((n, d), F32),
        compiler_params=_params("parallel"),
        name="mlp_ln",
    )(x2d, wu, wd, g, b)


def _fox_cumsum_kernel(f_ref, bias_ref, tri_ref, c_ref, *, blk):
    tri = tri_ref[...]
    carry = jnp.zeros((1, LANES), F32)
    for r0 in range(0, f_ref.shape[1], blk):
        z = f_ref[0, r0:r0 + blk, :] + bias_ref[...]
        lf = jnp.minimum(z, 0.0) - jnp.log1p(jnp.exp(-jnp.abs(z)))
        p0 = lf.astype(BF16)
        r1 = lf - p0.astype(F32)
        p1 = r1.astype(BF16)
        p2 = (r1 - p1.astype(F32)).astype(BF16)
        cs = _dot(tri, p0) + _dot(tri, p1) + _dot(tri, p2) + carry
        c_ref[0, r0:r0 + blk, :] = cs
        carry = cs[blk - 1:blk, :]


def _fox_cumsum(f3d, bias, blk=256):
    b, s, _ = f3d.shape
    blk = min(blk, s)
    tri = jnp.asarray(np.tril(np.ones((blk, blk), np.float32)), BF16)
    return pl.pallas_call(
        functools.partial(_fox_cumsum_kernel, blk=blk),
        grid=(b,),
        in_specs=[pl.BlockSpec((1, s, LANES), lambda i: (i, 0, 0)),
                  pl.BlockSpec((1, LANES), lambda i: (0, 0)),
                  pl.BlockSpec((blk, blk), lambda i: (0, 0))],
        out_specs=pl.BlockSpec((1, s, LANES), lambda i: (i, 0, 0)),
        out_shape=jax.ShapeDtypeStruct((b, s, LANES), F32),
        compiler_params=_params("parallel"),
        name="fox_cumsum",
    )(f3d, bias, tri)


def _flash_kernel(*refs, mode, tq, tk, window, lam_init):
    it = iter(refs)
    q_ref, k_ref, v_ref = next(it), next(it), next(it)
    if mode == "fox":
        ccol_ref, crow_ref = next(it), next(it)
    if mode == "diff":
        lam_ref, subln_ref = next(it), next(it)
    if mode == "sel":
        sel_ref, e_ref = next(it), next(it)
    if mode in ("sel", "win"):
        gate_ref, r_ref, prev_ref = next(it), next(it), next(it)
    o_ref, m_ref, l_ref, acc_ref = next(it), next(it), next(it), next(it)

    c = pl.program_id(1)
    qi = pl.program_id(2)
    q = q_ref[0]
    lane = lax.broadcasted_iota(jnp.int32, (tq, LANES), 1)
    lo_half = lane < HEAD_DIM
    zero = jnp.zeros_like(q)
    qpad = (jnp.where(lo_half, q, zero), jnp.where(lo_half, zero, q))

    m_ref[...] = jnp.full(m_ref.shape, NEG, F32)
    l_ref[...] = jnp.zeros(l_ref.shape, F32)
    acc_ref[...] = jnp.zeros(acc_ref.shape, F32)

    if mode == "fox":
        ct = ccol_ref[0]
        cq = [jnp.sum(jnp.where(lane == 2 * c + h, ct, 0.0), axis=-1, keepdims=True) for h in (0, 1)]

    def tile(kt, causal, win):
        k0 = pl.multiple_of(kt * tk, tk)
        kb = k_ref[0, pl.ds(k0, tk), :]
        vb = v_ref[0, pl.ds(k0, tk), :]
        if causal or win:
            rows = lax.broadcasted_iota(jnp.int32, (tq, tk), 0)
            cols = lax.broadcasted_iota(jnp.int32, (tq, tk), 1)
            dist = (qi * tq - k0) + rows - cols
        for h in (0, 1):
            s = _dot_nt(qpad[h], kb)
            if mode == "fox":
                ck = crow_ref[0, pl.ds(2 * c + h, 1), pl.ds(k0, tk)]
                s = s + (cq[h] - ck)
            if mode == "sel":
                s = jnp.where(_dot(sel_ref[0, h], e_ref[kt]) > 0.5, s, NEG)
            if causal:
                s = jnp.where(dist >= 0, s, NEG)
            if win:
                s = jnp.where(dist < window, s, NEG)
            m_prev = m_ref[h]
            m_new = jnp.maximum(m_prev, jnp.max(s, axis=-1, keepdims=True))
            alpha = jnp.exp(m_prev - m_new)
            p = jnp.exp(s - _rep_lanes(m_new, tk // LANES))
            l_ref[h] = alpha * l_ref[h] + jnp.sum(p, axis=-1, keepdims=True)
            acc_ref[h] = alpha * acc_ref[h] + _dot(p.astype(BF16), vb)
            m_ref[h] = m_new

    def body(kt, carry):
        tile(kt, False, mode == "win")
        return carry

    lo = jnp.maximum(qi - window // tk, 0) if mode == "win" else 0
    lax.fori_loop(lo, qi, body, 0)
    tile(qi, True, False)

    if mode == "diff":
        lp = lam_ref[...]
        lam = (jnp.exp(jnp.sum(lp[0:1] * lp[1:2], axis=-1, keepdims=True))
               - jnp.exp(jnp.sum(lp[2:3] * lp[3:4], axis=-1, keepdims=True)) + lam_init)
        o = acc_ref[0] / l_ref[0] - lam * (acc_ref[1] / l_ref[1])
        ms = jnp.mean(o * o, axis=-1, keepdims=True)
        o = o * lax.rsqrt(ms + LN_EPS) * subln_ref[...] * (1.0 - lam_init)
    else:
        o = jnp.where(lo_half, acc_ref[0] / l_ref[0], acc_ref[1] / l_ref[1])
    if mode in ("sel", "win"):
        o = prev_ref[0] + _gate_lanes(gate_ref[0], r_ref[0]) * o
    o_ref[0] = o.astype(o_ref.dtype)


def _gate_lanes(gate_logits, r):
    sig = jax.nn.sigmoid(gate_logits)
    hi = sig.astype(BF16)
    lo = (sig - hi.astype(F32)).astype(BF16)
    return _dot(hi, r) + _dot(lo, r)


def _flash(mode, q_arr, q_off, k_arr, k_off, v_arr, v_off, n_chunks, kv_per_chunk, out_dtype,
           extras=(), window=0, lam_init=0.0, tq=256):
    b, s, _ = q_arr.shape
    tq = min(tq, s)
    tk = tq
    nq = s // tq
    kv = (lambda off: (lambda bi, ci, qi: (bi, 0, off + ci))) if kv_per_chunk else \
         (lambda off: (lambda bi, ci, qi: (bi, 0, off)))
    in_specs = [pl.BlockSpec((1, tq, LANES), lambda bi, ci, qi: (bi, qi, q_off + ci)),
                pl.BlockSpec((1, s, LANES), kv(k_off)),
                pl.BlockSpec((1, s, LANES), kv(v_off))]
    args = [q_arr, k_arr, v_arr]
    for arr, spec in extras:
        args.append(arr)
        in_specs.append(spec)
    return pl.pallas_call(
        functools.partial(_flash_kernel, mode=mode, tq=tq, tk=tk, window=window, lam_init=lam_init),
        grid=(b, n_chunks, nq),
        in_specs=in_specs,
        out_specs=pl.BlockSpec((1, tq, LANES), lambda bi, ci, qi: (bi, qi, ci)),
        out_shape=jax.ShapeDtypeStruct((b, s, n_chunks * LANES), out_dtype),
        scratch_shapes=[pltpu.VMEM((2, tq, LANES), F32)] * 3,
        compiler_params=_params("parallel", "parallel", "arbitrary"),
        name="flash_" + mode,
    )(*args)


def _nsa_compress_kernel(tk_ref, tv_ref, w1a_ref, w1b_ref, pe_ref, b1_ref, w2_ref,
                         cos_ref, sa_ref, sb_ref, kc_ref, vc_ref):
    ncp = tk_ref.shape[1]
    for j, (t_ref, o_ref) in enumerate(((tk_ref, kc_ref), (tv_ref, vc_ref))):
        t = t_ref[0]
        u = _dot(t, w1a_ref[j])
        v = _dot(t, w1b_ref[j])
        const = _dot(pe_ref[j, 0], w1a_ref[j]) + _dot(pe_ref[j, 1], w1b_ref[j])
        pre = u + pltpu.roll(v, ncp - 1, 0) + const[0:1] + b1_ref[j]
        h = jax.nn.gelu(pre, approximate=True)
        out = _dot(h.astype(BF16), w2_ref[j])
        if j == 0:
            out = _rope(out, cos_ref[...], sa_ref[...], sb_ref[...])
        o_ref[0] = out.astype(o_ref.dtype)


def _nsa_compress(tkc, tvc, w1a, w1b, pe, b1, w2, tables_c):
    b, ncp, width = tkc.shape
    full = lambda a: pl.BlockSpec(a.shape, lambda i: (0,) * a.ndim)
    return pl.pallas_call(
        _nsa_compress_kernel,
        grid=(b,),
        in_specs=[pl.BlockSpec((1, ncp, width), lambda i: (i, 0, 0)),
                  pl.BlockSpec((1, ncp, width), lambda i: (i, 0, 0)),
                  full(w1a), full(w1b), full(pe), full(b1), full(w2)] + [full(t) for t in tables_c],
        out_specs=[pl.BlockSpec((1, ncp, LANES), lambda i: (i, 0, 0))] * 2,
        out_shape=[jax.ShapeDtypeStruct((b, ncp, LANES), BF16)] * 2,
        compiler_params=_params("parallel"),
        name="nsa_compress",
    )(tkc, tvc, w1a, w1b, pe, b1, w2, *tables_c)


def _nsa_cmp_kernel(q_ref, kc_ref, vc_ref, gate_ref, r_ref, ov_ref, oc_ref, sel_ref, *, tq, nsel, topk):
    qi = pl.program_id(1)
    kc = kc_ref[0]
    vc = vc_ref[0]
    ncp = kc.shape[0]
    ov = ov_ref[...]
    lane = lax.broadcasted_iota(jnp.int32, (tq, LANES), 1)
    lo_half = lane < HEAD_DIM
    tpos_c = qi * tq + lax.broadcasted_iota(jnp.int32, (tq, ncp), 0)
    ncol = lax.broadcasted_iota(jnp.int32, (tq, ncp), 1)
    valid = (NSA_CMP_STRIDE * ncol + NSA_CMP_LEN - 1) <= tpos_c
    tpos = qi * tq + lax.broadcasted_iota(jnp.int32, (tq, LANES), 0)
    cur = tpos // NSA_SEL_LEN
    gates = jax.nn.sigmoid(gate_ref[0])
    g_hi = gates.astype(BF16)
    g_lo = (gates - g_hi.astype(F32)).astype(BF16)

    o_chunks = [None] * NSA_HPG
    for g in range(NSA_KV_HEADS):
        in_half = lo_half if g == 0 else jnp.logical_not(lo_half)
        psum = jnp.zeros((tq, ncp), F32)
        for j in range(NSA_HPG):
            qc = q_ref[0, :, j * LANES:(j + 1) * LANES]
            s = _dot_nt(jnp.where(in_half, qc, jnp.zeros_like(qc)), kc)
            s = jnp.where(valid, s, NEG)
            e = jnp.exp(s - jnp.max(s, axis=-1, keepdims=True))
            p = jnp.where(valid, e / jnp.sum(e, axis=-1, keepdims=True), 0.0)
            psum = psum + p
            o = _dot(p.astype(BF16), vc)
            o_chunks[j] = o if g == 0 else jnp.where(lo_half, o_chunks[j], o)
        p_hi = psum.astype(BF16)
        p_lo = (psum - p_hi.astype(F32)).astype(BF16)
        imp = _dot(p_hi, ov) + _dot(p_lo, ov)
        imp = jnp.where(lane > cur, -NSA_BIG, imp)
        imp = jnp.where(lane == cur - 1, NSA_BIG, imp)
        imp = jnp.where(lane == cur, NSA_BIG, imp)
        imp = jnp.where(lane == 0, NSA_BIG, imp)
        imp = jnp.where(lane >= nsel, -3.0 * NSA_BIG, imp)
        rank = jnp.zeros((tq, LANES), jnp.int32)
        for jp in range(nsel):
            col = imp[:, jp:jp + 1]
            ahead = jnp.where(col > imp, 1, jnp.where(col == imp, jnp.where(lane > jp, 1, 0), 0))
            rank = rank + ahead
        sel_ref[0, g] = jnp.where(rank < topk, 1.0, 0.0).astype(sel_ref.dtype)
    for j in range(NSA_HPG):
        r = r_ref[j * NSA_N_BRANCH]
        oc_ref[0, :, j * LANES:(j + 1) * LANES] = (_dot(g_hi, r) + _dot(g_lo, r)) * o_chunks[j]


def _nsa_cmp(qk3d, q_blk, kc, vc, gate3d, r_tab, ov, nsel, tq=256):
    b, s, _ = qk3d.shape
    tq = min(tq, s)
    ncp = kc.shape[1]
    width = NSA_HPG * LANES
    full = lambda a: pl.BlockSpec(a.shape, lambda bi, qi: (0,) * a.ndim)
    return pl.pallas_call(
        functools.partial(_nsa_cmp_kernel, tq=tq, nsel=nsel, topk=min(NSA_TOPK, nsel)),
        grid=(b, s // tq),
        in_specs=[pl.BlockSpec((1, tq, width), lambda bi, qi: (bi, qi, q_blk)),
                  pl.BlockSpec((1, ncp, LANES), lambda bi, qi: (bi, 0, 0)),
                  pl.BlockSpec((1, ncp, LANES), lambda bi, qi: (bi, 0, 0)),
                  pl.BlockSpec((1, tq, LANES), lambda bi, qi: (bi, qi, 0)),
                  full(r_tab), full(ov)],
        out_specs=[pl.BlockSpec((1, tq, width), lambda bi, qi: (bi, qi, 0)),
                   pl.BlockSpec((1, NSA_KV_HEADS, tq, LANES), lambda bi, qi: (bi, 0, qi, 0))],
        out_shape=[jax.ShapeDtypeStruct((b, s, width), F32),
                   jax.ShapeDtypeStruct((b, NSA_KV_HEADS, s, LANES), BF16)],
        compiler_params=_params("parallel", "arbitrary"),
        name="nsa_cmp_topk",
    )(qk3d, kc, vc, gate3d, r_tab, ov)


def _rope_tables(pos):
    half = ROPE_DIM // 2
    inv = ROPE_THETA ** (-jnp.arange(half, dtype=F32) / half)
    ang = pos.astype(F32)[:, None] * inv[None, :]
    cos, sin = jnp.cos(ang), jnp.sin(ang)
    ones = jnp.ones((pos.shape[0], HEAD_DIM - ROPE_DIM), F32)
    zeros = jnp.zeros((pos.shape[0], HEAD_DIM - ROPE_DIM), F32)
    zh = jnp.zeros_like(sin)
    c64 = jnp.concatenate([cos, cos, ones], axis=1)
    sa64 = jnp.concatenate([-sin, zh, zeros], axis=1)
    sb64 = jnp.concatenate([zh, sin, zeros], axis=1)
    return tuple(jnp.concatenate([t, t], axis=1) for t in (c64, sa64, sb64))


def _even_in_weight(w):
    o = np.cumsum([0, 512, 512, 512, 512, 128, 128, 128, 128, 128, 128, 24])
    qa, ka, va, qn, kc, vc, ksl, vsl, kw, vw, g = (np.arange(o[i], o[i + 1]) for i in range(11))
    qn_perm = np.concatenate([qn[(gi * NSA_HPG + j) * HEAD_DIM:(gi * NSA_HPG + j + 1) * HEAD_DIM]
                              for j in range(NSA_HPG) for gi in range(NSA_KV_HEADS)])
    cols = np.concatenate([qa, ka, qn_perm, ksl, kw, va, vsl, vw, kc, vc, g])
    scale = np.ones(cols.shape[0], np.float32)
    scale[0:512] = HEAD_DIM ** -0.5
    scale[1024:1536] = HEAD_DIM ** -0.5
    wp = jnp.take(w, jnp.asarray(cols), axis=1) * jnp.asarray(scale)[None, :]
    wp = jnp.pad(wp, ((0, 0), (0, LANES - g.shape[0])))
    return wp.astype(BF16)


def _even_out_weight(w):
    rows_b = np.concatenate([512 + np.arange((gi * NSA_HPG + j) * HEAD_DIM, (gi * NSA_HPG + j + 1) * HEAD_DIM)
                             for j in range(NSA_HPG) for gi in range(NSA_KV_HEADS)])
    return w[:512].astype(BF16), jnp.take(w, jnp.asarray(rows_b), axis=0).astype(BF16)


def _odd_in_weight(w):
    scale = np.ones(w.shape[1], np.float32)
    scale[:D_MODEL] = HEAD_DIM ** -0.5
    wp = w * jnp.asarray(scale)[None, :]
    return jnp.pad(wp, ((0, 0), (0, LANES - FOX_HEADS))).astype(BF16)


def _compress_weights(pe, w1, b1, w2):
    eye = jnp.eye(NSA_KV_HEADS, dtype=F32)
    w1r = w1.reshape(2, 2, NSA_CMP_STRIDE, HEAD_DIM, NSA_CMP_HIDDEN)
    w1x = jnp.einsum("jardc,gh->jarghdc", w1r, eye)
    w1x = jnp.einsum("jarghdc->jargdhc", w1x).reshape(2, 2, NSA_CMP_STRIDE * LANES, NSA_KV_HEADS * NSA_CMP_HIDDEN)
    pe_r = pe.reshape(2, 2, NSA_CMP_STRIDE, 1, HEAD_DIM)
    pe_x = jnp.broadcast_to(pe_r, (2, 2, NSA_CMP_STRIDE, NSA_KV_HEADS, HEAD_DIM)).reshape(2, 2, 1, NSA_CMP_STRIDE * LANES)
    pe_x = jnp.broadcast_to(pe_x, (2, 2, 8, NSA_CMP_STRIDE * LANES))
    b1x = jnp.tile(b1, (1, NSA_KV_HEADS)).reshape(2, 1, NSA_KV_HEADS * NSA_CMP_HIDDEN)
    w2x = jnp.einsum("jcd,gh->jgchd", w2, eye).reshape(2, NSA_KV_HEADS * NSA_CMP_HIDDEN, LANES)
    return (w1x[:, 0].astype(BF16), w1x[:, 1].astype(BF16), pe_x.astype(BF16), b1x.astype(F32), w2x.astype(BF16))


def _nsa_constants(seq, tk):
    ncp = seq // NSA_CMP_STRIDE
    nsel = seq // NSA_SEL_LEN
    n = np.arange(ncp)[:, None]
    j = np.arange(LANES)[None, :]
    cs, ce = n * NSA_CMP_STRIDE, n * NSA_CMP_STRIDE + NSA_CMP_LEN - 1
    ss = j * NSA_SEL_LEN
    ov = ((cs <= ss + NSA_SEL_LEN - 1) & (ce >= ss) & (j < nsel) & (n < ncp - 1)).astype(np.float32)
    tok = np.arange(seq)
    e = (tok[None, :] // NSA_SEL_LEN == np.arange(LANES)[:, None]).astype(np.float32)
    e = e.reshape(LANES, seq // tk, tk).transpose(1, 0, 2)
    r = np.zeros((NSA_HPG * NSA_N_BRANCH, LANES, LANES), np.float32)
    for jc in range(NSA_HPG):
        for br in range(NSA_N_BRANCH):
            for ln in range(LANES):
                head = (ln // HEAD_DIM) * NSA_HPG + jc
                r[jc * NSA_N_BRANCH + br, head * NSA_N_BRANCH + br, ln] = 1.0
    return jnp.asarray(ov, BF16), jnp.asarray(e, BF16), jnp.asarray(r, BF16), nsel


def kernel(x, ln_gain, ln_bias, mlp_w_up, mlp_w_down, w_in_even, w_out_even, diff_lambda, diff_subln,
           nsa_pe, nsa_cmp_w1, nsa_cmp_b1, nsa_cmp_w2, w_in_odd, fox_f_bias, w_out_odd):
    b, s, d = x.shape
    n = b * s
    tq = min(256, s)
    ncp = s // NSA_CMP_STRIDE
    xf = x.reshape(n, d)
    tables = _rope_tables(jnp.arange(s))
    tables_c = _rope_tables(jnp.arange(ncp) * NSA_CMP_STRIDE + NSA_CMP_LEN - 1)
    ov, e_tab, r_tab, nsel = _nsa_constants(s, tq)
    row = lambda a: a.reshape(1, -1).astype(F32)

    for layer in range(DEPTH):
        li = layer // 2
        if layer % 2 == 0:
            qk, vv, kci, vci, gate = _proj(xf, _even_in_weight(w_in_even[li]), tables, EVEN_SEGS, EVEN_DTYPES, s)
            qk3, vv3, gate3 = qk.reshape(b, s, -1), vv.reshape(b, s, -1), gate.reshape(b, s, LANES)
            lam_init = 0.8 - 0.6 * math.exp(-0.3 * layer)
            full2 = lambda a: pl.BlockSpec(a.shape, lambda bi, ci, qi: (0, 0))
            lam_p, subln = diff_lambda[li].astype(F32), row(diff_subln[li])
            o_a = _flash("diff", qk3, 0, qk3, 4, vv3, 0, DIFF_HEADS, True, BF16,
                         extras=((lam_p, full2(lam_p)), (subln, full2(subln))), lam_init=lam_init, tq=tq)
            w1a, w1b, pe_x, b1x, w2x = _compress_weights(nsa_pe[li], nsa_cmp_w1[li], nsa_cmp_b1[li], nsa_cmp_w2[li])
            kc, vc = _nsa_compress(kci.reshape(b, ncp, -1), vci.reshape(b, ncp, -1), w1a, w1b, pe_x, b1x, w2x, tables_c)
            o_c, sel = _nsa_cmp(qk3, 2, kc, vc, gate3, r_tab, ov, nsel, tq=tq)
            gate_spec = pl.BlockSpec((1, tq, LANES), lambda bi, ci, qi: (bi, qi, 0))
            prev_spec = pl.BlockSpec((1, tq, LANES), lambda bi, ci, qi: (bi, qi, ci))
            r_spec = lambda br: pl.BlockSpec((1, LANES, LANES), lambda bi, ci, qi: (ci * NSA_N_BRANCH + br, 0, 0))
            sel_spec = pl.BlockSpec((1, NSA_KV_HEADS, tq, LANES), lambda bi, ci, qi: (bi, 0, qi, 0))
            e_spec = pl.BlockSpec(e_tab.shape, lambda bi, ci, qi: (0, 0, 0))
            o_cs = _flash("sel", qk3, 8, qk3, 12, vv3, 4, NSA_HPG, False, F32,
                          extras=((sel, sel_spec), (e_tab, e_spec), (gate3, gate_spec), (r_tab, r_spec(1)),
                                  (o_c, prev_spec)), tq=tq)
            o_b = _flash("win", qk3, 8, qk3, 13, vv3, 5, NSA_HPG, False, BF16,
                         extras=((gate3, gate_spec), (r_tab, r_spec(2)), (o_cs, prev_spec)),
                         window=NSA_WINDOW, tq=tq)
            wo_a, wo_b = _even_out_weight(w_out_even[li])
            xf = _outproj_ln([o_a.reshape(n, -1), o_b.reshape(n, -1)], [wo_a, wo_b], xf,
                             row(ln_gain[layer, 0]), row(ln_bias[layer, 0]))
        else:
            qkv, fl = _proj(xf, _odd_in_weight(w_in_odd[li]), None, ODD_SEGS, ODD_DTYPES, s)
            qkv3 = qkv.reshape(b, s, -1)
            bias = jnp.pad(fox_f_bias[li].astype(F32), (0, LANES - FOX_HEADS)).reshape(1, LANES)
            c_col = _fox_cumsum(fl.reshape(b, s, LANES), bias)
            c_row = jnp.transpose(c_col[:, :, :FOX_HEADS], (0, 2, 1))
            ccol_spec = pl.BlockSpec((1, tq, LANES), lambda bi, ci, qi: (bi, qi, 0))
            crow_spec = pl.BlockSpec((1, FOX_HEADS, s), lambda bi, ci, qi: (bi, 0, 0))
            nch = FOX_HEADS // 2
            o_f = _flash("fox", qkv3, 0, qkv3, nch, qkv3, 2 * nch, nch, True, BF16,
                         extras=((c_col, ccol_spec), (c_row, crow_spec)), tq=tq)
            xf = _outproj_ln([o_f.reshape(n, -1)], [w_out_odd[li].astype(BF16)], xf,
                             row(ln_gain[layer, 0]), row(ln_bias[layer, 0]))
        xf = _mlp_ln(xf, mlp_w_up[layer].astype(BF16), mlp_w_down[layer].astype(BF16),
                     row(ln_gain[layer, 1]), row(ln_bias[layer, 1]))
    return xf.reshape(b, s, d)
```

```python
import functools
import math

import numpy as np
import jax
import jax.numpy as jnp
from jax import lax
from jax.experimental import pallas as pl
from jax.experimental.pallas import tpu as pltpu

F32 = jnp.float32
BF16 = jnp.bfloat16

D_MODEL = 1024
DEPTH = 4
HEAD_DIM = 64
ROPE_DIM = HEAD_DIM // 4
ROPE_THETA = 500000.0
LN_EPS = 1e-5
DIFF_HEADS = 4
NSA_HEADS = 8
NSA_KV_HEADS = 2
NSA_HPG = NSA_HEADS // NSA_KV_HEADS
NSA_CMP_LEN = 32
NSA_CMP_STRIDE = 16
NSA_CMP_HIDDEN = 256
NSA_SEL_LEN = 64
NSA_TOPK = 8
NSA_WINDOW = 512
NSA_N_BRANCH = 3
NSA_BIG = 1e4
FOX_HEADS = D_MODEL // HEAD_DIM
D_FF = 4 * D_MODEL
ALPHA = (2 * DEPTH) ** 0.25

LANES = 128
NEG = -1e30
VMEM_LIMIT = 56 * 1024 * 1024
LOG2E = 1.4426950408889634
MASK_BIG = 30000.0
FOX_AUG_ROWS = 16
FLASH_TQ = 2048

EVEN_SEGS = ((0, 1024, True), (1024, 768, True), (1792, 768, False),
             (2560, 128, False), (2688, 128, False), (2816, 128, False))
EVEN_DTYPES = (BF16, BF16, BF16, BF16, BF16, F32)
ODD_SEGS = ((0, D_MODEL, False), (D_MODEL, D_MODEL, False), (2 * D_MODEL, D_MODEL, False),
            (3 * D_MODEL, 128, False))
ODD_DTYPES = (BF16, BF16, BF16, F32)


def _dot(a, b):
    return jnp.dot(a, b, preferred_element_type=F32)


def _dot_nt(a, b):
    return lax.dot_general(a, b, (((1,), (1,)), ((), ())), preferred_element_type=F32)


def _rep_lanes(x, n):
    return x if n == 1 else jnp.concatenate([x] * n, axis=1)


def _split3(x):
    p0 = x.astype(BF16)
    r1 = x - p0.astype(F32)
    p1 = r1.astype(BF16)
    p2 = (r1 - p1.astype(F32)).astype(BF16)
    return p0, p1, p2


def _rope(h, cos, sa, sb):
    return h * cos + pltpu.roll(h, LANES - ROPE_DIM // 2, 1) * sa + pltpu.roll(h, ROPE_DIM // 2, 1) * sb


def _layer_norm(y, g, b):
    mu = jnp.mean(y, axis=-1, keepdims=True)
    yc = y - mu
    var = jnp.mean(yc * yc, axis=-1, keepdims=True)
    return yc * lax.rsqrt(var + LN_EPS) * g + b


def _params(*sem):
    return pltpu.CompilerParams(dimension_semantics=sem, vmem_limit_bytes=VMEM_LIMIT)


def _proj_kernel(*refs, segs, has_rope):
    x_ref, w_ref = refs[0], refs[1]
    pos = 2
    if has_rope:
        cos_ref, sa_ref, sb_ref = refs[2:5]
        pos = 5
    out_refs = refs[pos:]
    xb = x_ref[...].astype(BF16)
    for o_ref, (c0, width, rope) in zip(out_refs, segs):
        for cc in range(0, width, 512):
            w = min(512, width - cc)
            h = _dot(xb, w_ref[:, c0 + cc:c0 + cc + w])
            if rope:
                cos, sa, sb = cos_ref[...], sa_ref[...], sb_ref[...]
                for t in range(0, w, LANES):
                    o_ref[:, cc + t:cc + t + LANES] = _rope(h[:, t:t + LANES], cos, sa, sb).astype(o_ref.dtype)
            else:
                o_ref[:, cc:cc + w] = h.astype(o_ref.dtype)


def _proj(x2d, w, tables, segs, dtypes, seq, tm=512):
    n, d = x2d.shape
    m = w.shape[1]
    ns = seq // tm
    has_rope = tables is not None
    in_specs = [pl.BlockSpec((tm, d), lambda i: (i, 0)), pl.BlockSpec((d, m), lambda i: (0, 0))]
    args = [x2d, w]
    if has_rope:
        in_specs += [pl.BlockSpec((tm, LANES), lambda i: (i % ns, 0))] * 3
        args += list(tables)
    return pl.pallas_call(
        functools.partial(_proj_kernel, segs=segs, has_rope=has_rope),
        grid=(n // tm,),
        in_specs=in_specs,
        out_specs=[pl.BlockSpec((tm, s[1]), lambda i: (i, 0)) for s in segs],
        out_shape=[jax.ShapeDtypeStruct((n, s[1]), dt) for s, dt in zip(segs, dtypes)],
        compiler_params=_params("parallel"),
        name="in_proj",
    )(*args)


def _outproj_ln_kernel(*refs, n_in):
    o_refs = refs[:n_in]
    w_refs = refs[n_in:2 * n_in]
    x_ref, g_ref, b_ref, y_ref = refs[2 * n_in:]
    mix = _dot(o_refs[0][...], w_refs[0][...])
    for o_ref, w_ref in zip(o_refs[1:], w_refs[1:]):
        mix = mix + _dot(o_ref[...], w_ref[...])
    y_ref[...] = _layer_norm(ALPHA * x_ref[...] + mix, g_ref[...], b_ref[...])


def _outproj_ln(os_, ws, x2d, g, b, tm=512):
    n, d = x2d.shape
    in_specs = [pl.BlockSpec((tm, o.shape[1]), lambda i: (i, 0)) for o in os_]
    in_specs += [pl.BlockSpec(w.shape, lambda i: (0, 0)) for w in ws]
    in_specs += [pl.BlockSpec((tm, d), lambda i: (i, 0)),
                 pl.BlockSpec((1, d), lambda i: (0, 0)), pl.BlockSpec((1, d), lambda i: (0, 0))]
    return pl.pallas_call(
        functools.partial(_outproj_ln_kernel, n_in=len(os_)),
        grid=(n // tm,),
        in_specs=in_specs,
        out_specs=pl.BlockSpec((tm, d), lambda i: (i, 0)),
        out_shape=jax.ShapeDtypeStruct((n, d), F32),
        compiler_params=_params("parallel"),
        name="out_proj_ln",
    )(*os_, *ws, x2d, g, b)


def _mlp_ln_kernel(x_ref, wu_ref, wd_ref, g_ref, b_ref, y_ref, *, ff_chunk):
    x = x_ref[...]
    xb = x.astype(BF16)
    acc = None
    for c0 in range(0, wu_ref.shape[1], ff_chunk):
        h = jnp.maximum(_dot(xb, wu_ref[:, c0:c0 + ff_chunk]), 0.0)
        part = _dot((h * h).astype(BF16), wd_ref[c0:c0 + ff_chunk, :])
        acc = part if acc is None else acc + part
    y_ref[...] = _layer_norm(ALPHA * x + acc, g_ref[...], b_ref[...])


def _mlp_ln(x2d, wu, wd, g, b, tm=512, ff_chunk=1024):
    n, d = x2d.shape
    resident = pl.Buffered(1)
    return pl.pallas_call(
        functools.partial(_mlp_ln_kernel, ff_chunk=ff_chunk),
        grid=(n // tm,),
        in_specs=[pl.BlockSpec((tm, d), lambda i: (i, 0)),
                  pl.BlockSpec(wu.shape, lambda i: (0, 0), pipeline_mode=resident),
                  pl.BlockSpec(wd.shape, lambda i: (0, 0), pipeline_mode=resident),
                  pl.BlockSpec((1, d), lambda i: (0, 0)), pl.BlockSpec((1, d), lambda i: (0, 0))],
        out_specs=pl.BlockSpec((tm, d), lambda i: (i, 0)),
        out_shape=jax.ShapeDtypeStruct((n, d), F32),
        compiler_params=_params("parallel"),
        name="mlp_ln",
    )(x2d, wu, wd, g, b)


def _fox_cumsum_kernel(f_ref, bias_ref, tri_ref, c_ref, cp_ref, *, blk):
    tri = tri_ref[...]
    carry = jnp.zeros((1, LANES), F32)
    for r0 in range(0, f_ref.shape[1], blk):
        z = f_ref[0, r0:r0 + blk, :] + bias_ref[...]
        lf = jnp.minimum(z, 0.0) - jnp.log1p(jnp.exp(-jnp.abs(z)))
        p0, p1, p2 = _split3(lf)
        cs = _dot(tri, p0) + _dot(tri, p1) + _dot(tri, p2) + carry
        c2 = cs * LOG2E
        c_ref[0, r0:r0 + blk, :] = c2
        for i, piece in enumerate(_split3(c2)):
            cp_ref[i, 0, r0:r0 + blk, :] = piece
        carry = cs[blk - 1:blk, :]


def _fox_cumsum(f3d, bias, blk=256):
    b, s, _ = f3d.shape
    blk = min(blk, s)
    tri = jnp.asarray(np.tril(np.ones((blk, blk), np.float32)), BF16)
    return pl.pallas_call(
        functools.partial(_fox_cumsum_kernel, blk=blk),
        grid=(b,),
        in_specs=[pl.BlockSpec((1, s, LANES), lambda i: (i, 0, 0)),
                  pl.BlockSpec((1, LANES), lambda i: (0, 0)),
                  pl.BlockSpec((blk, blk), lambda i: (0, 0))],
        out_specs=[pl.BlockSpec((1, s, LANES), lambda i: (i, 0, 0)),
                   pl.BlockSpec((3, 1, s, LANES), lambda i: (0, i, 0, 0))],
        out_shape=[jax.ShapeDtypeStruct((b, s, LANES), F32),
                   jax.ShapeDtypeStruct((3, b, s, LANES), BF16)],
        compiler_params=_params("parallel"),
        name="fox_cumsum",
    )(f3d, bias, tri)


def _flash_kernel(*refs, mode, tq, tk, rc, window, lam_init):
    it = iter(refs)
    q_ref, kt_ref, v_ref = next(it), next(it), next(it)
    if mode == "fox":
        ccol_ref, caug_ref = next(it), next(it)
    if mode == "diff":
        lam_ref, subln_ref = next(it), next(it)
    if mode == "sel":
        nsel_ref, eneg_ref = next(it), next(it)
    if mode in ("sel", "win"):
        gate_ref, r_ref, prev_ref = next(it), next(it), next(it)
    o_ref, qa_ref, m_ref, acc_ref = next(it), next(it), next(it), next(it)

    c = pl.program_id(1)
    qi = pl.program_id(2)
    acc_w = acc_ref.shape[1]
    q = q_ref[0]
    lane = lax.broadcasted_iota(jnp.int32, (tq, LANES), 1)
    lo_half = lane < HEAD_DIM
    zero = jnp.zeros_like(q)
    qa_ref[0:tq, 0:LANES] = jnp.where(lo_half, q, zero)
    qa_ref[tq:2 * tq, 0:LANES] = jnp.where(lo_half, zero, q)
    if mode == "fox":
        ct = ccol_ref[0]
        for h in (0, 1):
            cq = jnp.sum(jnp.where(lane == 2 * c + h, ct, 0.0), axis=-1, keepdims=True)
            a0, a1, a2 = (p.astype(F32) for p in _split3(cq))
            aug = jnp.where(lane == 0, a0, jnp.where(lane == 1, a1, jnp.where(lane == 2, a2, 0.0)))
            aug = jnp.where(lane >= 3 + 3 * h, jnp.where(lane < 6 + 3 * h, -1.0, aug), aug)
            qa_ref[h * tq:(h + 1) * tq, LANES:2 * LANES] = aug.astype(BF16)
    if mode == "sel":
        for h in (0, 1):
            qa_ref[h * tq:(h + 1) * tq, LANES:2 * LANES] = nsel_ref[0, h]
    m_ref[...] = jnp.full(m_ref.shape, NEG, F32)
    acc_ref[...] = jnp.zeros(acc_ref.shape, F32)

    lane_k = lax.broadcasted_iota(jnp.int32, (tk, LANES), 1)
    lo_half_k = lane_k < HEAD_DIM

    nsub = tq // tk

    def tile(kt, d, chunks):
        k0 = pl.multiple_of(kt * tk, tk)
        kb = kt_ref[0, :, pl.ds(k0, tk)]
        if mode == "fox":
            kb = jnp.concatenate([kb, caug_ref[0, 0, :, pl.ds(k0, tk)],
                                  jnp.zeros((LANES - FOX_AUG_ROWS, tk), BF16)], axis=0)
        if mode == "sel":
            kb = jnp.concatenate([kb, eneg_ref[:, pl.ds(k0, tk)]], axis=0)
        vb = v_ref[0, pl.ds(k0, tk), :]
        one = jnp.ones_like(vb)
        if mode == "diff":
            vaug = (jnp.concatenate([vb, one], axis=1),) * 2
        else:
            vaug = (jnp.where(lo_half_k, vb, one), jnp.where(lo_half_k, one, vb))
        for h in (0, 1):
            for rr, causal, win in chunks:
                r0 = h * tq + rr
                s = _dot(qa_ref[r0:r0 + rc, :], kb)
                if causal or win:
                    rows = lax.broadcasted_iota(jnp.int32, (rc, tk), 0)
                    cols = lax.broadcasted_iota(jnp.int32, (rc, tk), 1)
                    dist = (rr - d * tk) + rows - cols
                    if causal:
                        s = jnp.where(dist >= 0, s, NEG)
                    if win:
                        s = jnp.where(dist < window, s, NEG)
                m_prev = m_ref[r0:r0 + rc, :]
                m_new = jnp.maximum(m_prev, jnp.max(s, axis=-1, keepdims=True))
                p = jnp.exp2(s - _rep_lanes(m_new, tk // LANES)).astype(BF16)
                alpha = jnp.exp2(m_prev - m_new)
                acc_ref[r0:r0 + rc, :] = (_rep_lanes(alpha, acc_w // LANES) * acc_ref[r0:r0 + rc, :]
                                          + _dot(p, vaug[h]))
                m_ref[r0:r0 + rc, :] = m_new

    all_rows = range(0, tq, rc)
    if mode == "win":
        n_back = window // tk
        for d in range(-n_back, nsub):
            chunks = [(rr, rr // tk == d, rr // tk == d + n_back) for rr in all_rows if d <= rr // tk <= d + n_back]
            if d < 0:
                @pl.when(qi * nsub + d >= 0)
                def _():
                    tile(qi * nsub + d, d, chunks)
            else:
                tile(qi * nsub + d, d, chunks)
    else:
        def body(kt, carry):
            tile(kt, 0, [(rr, False, False) for rr in all_rows])
            return carry
        lax.fori_loop(0, qi * nsub, body, 0)
        for d in range(nsub):
            tile(qi * nsub + d, d, [(rr, rr // tk == d, False) for rr in all_rows if rr // tk >= d])

    if mode == "diff":
        lp = lam_ref[...]
        lam = (jnp.exp(jnp.sum(lp[0:1] * lp[1:2], axis=-1, keepdims=True))
               - jnp.exp(jnp.sum(lp[2:3] * lp[3:4], axis=-1, keepdims=True)) + lam_init)
        o = (acc_ref[0:tq, 0:LANES] / acc_ref[0:tq, LANES:2 * LANES]
             - lam * (acc_ref[tq:2 * tq, 0:LANES] / acc_ref[tq:2 * tq, LANES:2 * LANES]))
        ms = jnp.mean(o * o, axis=-1, keepdims=True)
        o = o * lax.rsqrt(ms + LN_EPS) * subln_ref[...] * (1.0 - lam_init)
    else:
        acc0, acc1 = acc_ref[0:tq, :], acc_ref[tq:2 * tq, :]
        o = jnp.where(lo_half, acc0 / pltpu.roll(acc0, HEAD_DIM, 1), acc1 / pltpu.roll(acc1, HEAD_DIM, 1))
    if mode in ("sel", "win"):
        o = prev_ref[0] + _gate_lanes(gate_ref[0], r_ref[0]) * o
    o_ref[0] = o.astype(o_ref.dtype)


def _gate_lanes(gate_logits, r):
    sig = jax.nn.sigmoid(gate_logits)
    hi = sig.astype(BF16)
    lo = (sig - hi.astype(F32)).astype(BF16)
    return _dot(hi, r) + _dot(lo, r)


def _flash(mode, q_arr, q_off, kt_arr, k_off, v_arr, v_off, n_chunks, kv_per_chunk, out_dtype,
           extras=(), window=0, lam_init=0.0, tq=256, tk=256, rc=128):
    b, s, _ = q_arr.shape
    tq = min(tq, s)
    tk = min(tk, tq)
    rc = min(rc, tk)
    nq = s // tq
    kt_map = (lambda bi, ci, qi: (bi, k_off + ci, 0)) if kv_per_chunk else (lambda bi, ci, qi: (bi, k_off, 0))
    v_map = (lambda bi, ci, qi: (bi, 0, v_off + ci)) if kv_per_chunk else (lambda bi, ci, qi: (bi, 0, v_off))
    in_specs = [pl.BlockSpec((1, tq, LANES), lambda bi, ci, qi: (bi, qi, q_off + ci)),
                pl.BlockSpec((1, LANES, s), kt_map),
                pl.BlockSpec((1, s, LANES), v_map)]
    args = [q_arr, kt_arr, v_arr]
    for arr, spec in extras:
        args.append(arr)
        in_specs.append(spec)
    kdim = 2 * LANES if mode in ("fox", "sel") else LANES
    acc_w = 2 * LANES if mode == "diff" else LANES
    return pl.pallas_call(
        functools.partial(_flash_kernel, mode=mode, tq=tq, tk=tk, rc=rc, window=window, lam_init=lam_init),
        grid=(b, n_chunks, nq),
        in_specs=in_specs,
        out_specs=pl.BlockSpec((1, tq, LANES), lambda bi, ci, qi: (bi, qi, ci)),
        out_shape=jax.ShapeDtypeStruct((b, s, n_chunks * LANES), out_dtype),
        scratch_shapes=[pltpu.VMEM((2 * tq, kdim), BF16), pltpu.VMEM((2 * tq, LANES), F32),
                        pltpu.VMEM((2 * tq, acc_w), F32)],
        compiler_params=_params("parallel", "parallel", "arbitrary"),
        name="flash_" + mode,
    )(*args)


def _nsa_compress_kernel(tk_ref, tv_ref, w1a_ref, w1b_ref, pe_ref, b1_ref, w2_ref,
                         cos_ref, sa_ref, sb_ref, kc_ref, vc_ref):
    ncp = tk_ref.shape[1]
    for j, (t_ref, o_ref) in enumerate(((tk_ref, kc_ref), (tv_ref, vc_ref))):
        t = t_ref[0]
        u = _dot(t, w1a_ref[j])
        v = _dot(t, w1b_ref[j])
        const = _dot(pe_ref[j, 0], w1a_ref[j]) + _dot(pe_ref[j, 1], w1b_ref[j])
        pre = u + pltpu.roll(v, ncp - 1, 0) + const[0:1] + b1_ref[j]
        h = jax.nn.gelu(pre, approximate=True)
        out = _dot(h.astype(BF16), w2_ref[j])
        if j == 0:
            out = _rope(out, cos_ref[...], sa_ref[...], sb_ref[...])
        o_ref[0] = out.astype(o_ref.dtype)


def _nsa_compress(tkc, tvc, w1a, w1b, pe, b1, w2, tables_c):
    b, ncp, width = tkc.shape
    full = lambda a: pl.BlockSpec(a.shape, lambda i: (0,) * a.ndim)
    return pl.pallas_call(
        _nsa_compress_kernel,
        grid=(b,),
        in_specs=[pl.BlockSpec((1, ncp, width), lambda i: (i, 0, 0)),
                  pl.BlockSpec((1, ncp, width), lambda i: (i, 0, 0)),
                  full(w1a), full(w1b), full(pe), full(b1), full(w2)] + [full(t) for t in tables_c],
        out_specs=[pl.BlockSpec((1, ncp, LANES), lambda i: (i, 0, 0))] * 2,
        out_shape=[jax.ShapeDtypeStruct((b, ncp, LANES), BF16)] * 2,
        compiler_params=_params("parallel"),
        name="nsa_compress",
    )(tkc, tvc, w1a, w1b, pe, b1, w2, *tables_c)


def _nsa_cmp_kernel(q_ref, kc_ref, vc_ref, gate_ref, r_ref, ov_ref, oc_ref, nsel_ref, *, tq, nsel, topk):
    qi = pl.program_id(1)
    kc = kc_ref[0]
    vc = vc_ref[0]
    ncp = kc.shape[0]
    ov = ov_ref[...]
    lane = lax.broadcasted_iota(jnp.int32, (tq, LANES), 1)
    lo_half = lane < HEAD_DIM
    tpos_c = qi * tq + lax.broadcasted_iota(jnp.int32, (tq, ncp), 0)
    ncol = lax.broadcasted_iota(jnp.int32, (tq, ncp), 1)
    valid = (NSA_CMP_STRIDE * ncol + NSA_CMP_LEN - 1) <= tpos_c
    tpos = qi * tq + lax.broadcasted_iota(jnp.int32, (tq, LANES), 0)
    cur = tpos // NSA_SEL_LEN
    gates = jax.nn.sigmoid(gate_ref[0])
    g_hi = gates.astype(BF16)
    g_lo = (gates - g_hi.astype(F32)).astype(BF16)

    o_chunks = [None] * NSA_HPG
    for g in range(NSA_KV_HEADS):
        in_half = lo_half if g == 0 else jnp.logical_not(lo_half)
        psum = jnp.zeros((tq, ncp), F32)
        for j in range(NSA_HPG):
            qc = q_ref[0, :, j * LANES:(j + 1) * LANES]
            s = _dot_nt(jnp.where(in_half, qc, jnp.zeros_like(qc)), kc)
            s = jnp.where(valid, s, NEG)
            e = jnp.exp2(s - jnp.max(s, axis=-1, keepdims=True))
            p = jnp.where(valid, e / jnp.sum(e, axis=-1, keepdims=True), 0.0)
            psum = psum + p
            o = _dot(p.astype(BF16), vc)
            o_chunks[j] = o if g == 0 else jnp.where(lo_half, o_chunks[j], o)
        p_hi = psum.astype(BF16)
        p_lo = (psum - p_hi.astype(F32)).astype(BF16)
        imp = _dot(p_hi, ov) + _dot(p_lo, ov)
        imp = jnp.where(lane > cur, -NSA_BIG, imp)
        imp = jnp.where(lane == cur - 1, NSA_BIG, imp)
        imp = jnp.where(lane == cur, NSA_BIG, imp)
        imp = jnp.where(lane == 0, NSA_BIG, imp)
        imp = jnp.where(lane >= nsel, -3.0 * NSA_BIG, imp)
        rank = jnp.zeros((tq, LANES), jnp.int32)
        for jp in range(nsel):
            col = imp[:, jp:jp + 1]
            ahead = jnp.where(col > imp, 1, jnp.where(col == imp, jnp.where(lane > jp, 1, 0), 0))
            rank = rank + ahead
        nsel_ref[0, g] = jnp.where(rank < topk, 0.0, 1.0).astype(nsel_ref.dtype)
    for j in range(NSA_HPG):
        r = r_ref[j * NSA_N_BRANCH]
        oc_ref[0, :, j * LANES:(j + 1) * LANES] = (_dot(g_hi, r) + _dot(g_lo, r)) * o_chunks[j]


def _nsa_cmp(q3d, q_blk, kc, vc, gate3d, r_tab, ov, nsel, tq=256):
    b, s, _ = q3d.shape
    tq = min(tq, s)
    ncp = kc.shape[1]
    width = NSA_HPG * LANES
    full = lambda a: pl.BlockSpec(a.shape, lambda bi, qi: (0,) * a.ndim)
    return pl.pallas_call(
        functools.partial(_nsa_cmp_kernel, tq=tq, nsel=nsel, topk=min(NSA_TOPK, nsel)),
        grid=(b, s // tq),
        in_specs=[pl.BlockSpec((1, tq, width), lambda bi, qi: (bi, qi, q_blk)),
                  pl.BlockSpec((1, ncp, LANES), lambda bi, qi: (bi, 0, 0)),
                  pl.BlockSpec((1, ncp, LANES), lambda bi, qi: (bi, 0, 0)),
                  pl.BlockSpec((1, tq, LANES), lambda bi, qi: (bi, qi, 0)),
                  full(r_tab), full(ov)],
        out_specs=[pl.BlockSpec((1, tq, width), lambda bi, qi: (bi, qi, 0)),
                   pl.BlockSpec((1, NSA_KV_HEADS, tq, LANES), lambda bi, qi: (bi, 0, qi, 0))],
        out_shape=[jax.ShapeDtypeStruct((b, s, width), F32),
                   jax.ShapeDtypeStruct((b, NSA_KV_HEADS, s, LANES), BF16)],
        compiler_params=_params("parallel", "arbitrary"),
        name="nsa_cmp_topk",
    )(q3d, kc, vc, gate3d, r_tab, ov)


def _rope_tables(pos):
    half = ROPE_DIM // 2
    inv = ROPE_THETA ** (-jnp.arange(half, dtype=F32) / half)
    ang = pos.astype(F32)[:, None] * inv[None, :]
    cos, sin = jnp.cos(ang), jnp.sin(ang)
    ones = jnp.ones((pos.shape[0], HEAD_DIM - ROPE_DIM), F32)
    zeros = jnp.zeros((pos.shape[0], HEAD_DIM - ROPE_DIM), F32)
    zh = jnp.zeros_like(sin)
    c64 = jnp.concatenate([cos, cos, ones], axis=1)
    sa64 = jnp.concatenate([-sin, zh, zeros], axis=1)
    sb64 = jnp.concatenate([zh, sin, zeros], axis=1)
    return tuple(jnp.concatenate([t, t], axis=1) for t in (c64, sa64, sb64))


def _even_in_weight(w):
    o = np.cumsum([0, 512, 512, 512, 512, 128, 128, 128, 128, 128, 128, 24])
    qa, ka, va, qn, kc, vc, ksl, vsl, kw, vw, g = (np.arange(o[i], o[i + 1]) for i in range(11))
    qn_perm = np.concatenate([qn[(gi * NSA_HPG + j) * HEAD_DIM:(gi * NSA_HPG + j + 1) * HEAD_DIM]
                              for j in range(NSA_HPG) for gi in range(NSA_KV_HEADS)])
    cols = np.concatenate([qa, qn_perm, ka, ksl, kw, va, vsl, vw, kc, vc, g])
    scale = np.ones(cols.shape[0], np.float32)
    scale[0:1024] = HEAD_DIM ** -0.5 * LOG2E
    wp = jnp.take(w, jnp.asarray(cols), axis=1) * jnp.asarray(scale)[None, :]
    wp = jnp.pad(wp, ((0, 0), (0, LANES - g.shape[0])))
    return wp.astype(BF16)


def _even_out_weight(w):
    rows_b = np.concatenate([512 + np.arange((gi * NSA_HPG + j) * HEAD_DIM, (gi * NSA_HPG + j + 1) * HEAD_DIM)
                             for j in range(NSA_HPG) for gi in range(NSA_KV_HEADS)])
    return w[:512].astype(BF16), jnp.take(w, jnp.asarray(rows_b), axis=0).astype(BF16)


def _odd_in_weight(w):
    scale = np.ones(w.shape[1], np.float32)
    scale[:D_MODEL] = HEAD_DIM ** -0.5 * LOG2E
    wp = w * jnp.asarray(scale)[None, :]
    return jnp.pad(wp, ((0, 0), (0, LANES - FOX_HEADS))).astype(BF16)


def _compress_weights(pe, w1, b1, w2):
    eye = jnp.eye(NSA_KV_HEADS, dtype=F32)
    w1r = w1.reshape(2, 2, NSA_CMP_STRIDE, HEAD_DIM, NSA_CMP_HIDDEN)
    w1x = jnp.einsum("jardc,gh->jargdhc", w1r, eye)
    w1x = w1x.reshape(2, 2, NSA_CMP_STRIDE * LANES, NSA_KV_HEADS * NSA_CMP_HIDDEN)
    pe_r = pe.reshape(2, 2, NSA_CMP_STRIDE, 1, HEAD_DIM)
    pe_x = jnp.broadcast_to(pe_r, (2, 2, NSA_CMP_STRIDE, NSA_KV_HEADS, HEAD_DIM)).reshape(2, 2, 1, NSA_CMP_STRIDE * LANES)
    pe_x = jnp.broadcast_to(pe_x, (2, 2, 8, NSA_CMP_STRIDE * LANES))
    b1x = jnp.tile(b1, (1, NSA_KV_HEADS)).reshape(2, 1, NSA_KV_HEADS * NSA_CMP_HIDDEN)
    w2x = jnp.einsum("jcd,gh->jgchd", w2, eye).reshape(2, NSA_KV_HEADS * NSA_CMP_HIDDEN, LANES)
    return (w1x[:, 0].astype(BF16), w1x[:, 1].astype(BF16), pe_x.astype(BF16), b1x.astype(F32), w2x.astype(BF16))


def _nsa_constants(seq):
    ncp = seq // NSA_CMP_STRIDE
    nsel = seq // NSA_SEL_LEN
    n = np.arange(ncp)[:, None]
    j = np.arange(LANES)[None, :]
    cs, ce = n * NSA_CMP_STRIDE, n * NSA_CMP_STRIDE + NSA_CMP_LEN - 1
    ss = j * NSA_SEL_LEN
    ov = ((cs <= ss + NSA_SEL_LEN - 1) & (ce >= ss) & (j < nsel) & (n < ncp - 1)).astype(np.float32)
    tok = np.arange(seq)
    eneg = -MASK_BIG * (tok[None, :] // NSA_SEL_LEN == np.arange(LANES)[:, None]).astype(np.float32)
    r = np.zeros((NSA_HPG * NSA_N_BRANCH, LANES, LANES), np.float32)
    for jc in range(NSA_HPG):
        for br in range(NSA_N_BRANCH):
            for ln in range(LANES):
                head = (ln // HEAD_DIM) * NSA_HPG + jc
                r[jc * NSA_N_BRANCH + br, head * NSA_N_BRANCH + br, ln] = 1.0
    return jnp.asarray(ov, BF16), jnp.asarray(eneg, BF16), jnp.asarray(r, BF16), nsel


def _fox_key_aug(c_pieces):
    _, b, s, _ = c_pieces.shape
    pieces = jnp.transpose(c_pieces[..., :FOX_HEADS], (1, 3, 0, 2))
    pieces = pieces.reshape(b, FOX_HEADS // 2, 6, s)
    ones = jnp.ones((b, FOX_HEADS // 2, 3, s), BF16)
    pad = jnp.zeros((b, FOX_HEADS // 2, FOX_AUG_ROWS - 9, s), BF16)
    return jnp.concatenate([ones, pieces, pad], axis=2)


def kernel(x, ln_gain, ln_bias, mlp_w_up, mlp_w_down, w_in_even, w_out_even, diff_lambda, diff_subln,
           nsa_pe, nsa_cmp_w1, nsa_cmp_b1, nsa_cmp_w2, w_in_odd, fox_f_bias, w_out_odd):
    b, s, d = x.shape
    n = b * s
    tq = min(FLASH_TQ, s)
    ncp = s // NSA_CMP_STRIDE
    xf = x.reshape(n, d)
    tables = _rope_tables(jnp.arange(s))
    tables_c = _rope_tables(jnp.arange(ncp) * NSA_CMP_STRIDE + NSA_CMP_LEN - 1)
    ov, eneg, r_tab, nsel = _nsa_constants(s)
    row = lambda a: a.reshape(1, -1).astype(F32)
    t3 = lambda a: jnp.transpose(a.reshape(b, s, -1), (0, 2, 1))

    for layer in range(DEPTH):
        li = layer // 2
        if layer % 2 == 0:
            q, k, vv, kci, vci, gate = _proj(xf, _even_in_weight(w_in_even[li]), tables, EVEN_SEGS, EVEN_DTYPES, s)
            q3, kt3, vv3, gate3 = q.reshape(b, s, -1), t3(k), vv.reshape(b, s, -1), gate.reshape(b, s, LANES)
            lam_init = 0.8 - 0.6 * math.exp(-0.3 * layer)
            full2 = lambda a: pl.BlockSpec(a.shape, lambda bi, ci, qi: (0, 0))
            lam_p, subln = diff_lambda[li].astype(F32), row(diff_subln[li])
            o_a = _flash("diff", q3, 0, kt3, 0, vv3, 0, DIFF_HEADS, True, BF16,
                         extras=((lam_p, full2(lam_p)), (subln, full2(subln))), lam_init=lam_init, tq=tq)
            w1a, w1b, pe_x, b1x, w2x = _compress_weights(nsa_pe[li], nsa_cmp_w1[li], nsa_cmp_b1[li], nsa_cmp_w2[li])
            kc, vc = _nsa_compress(kci.reshape(b, ncp, -1), vci.reshape(b, ncp, -1), w1a, w1b, pe_x, b1x, w2x, tables_c)
            o_c, notsel = _nsa_cmp(q3, 1, kc, vc, gate3, r_tab, ov, nsel)
            gate_spec = pl.BlockSpec((1, tq, LANES), lambda bi, ci, qi: (bi, qi, 0))
            prev_spec = pl.BlockSpec((1, tq, LANES), lambda bi, ci, qi: (bi, qi, ci))
            r_spec = lambda br: pl.BlockSpec((1, LANES, LANES), lambda bi, ci, qi: (ci * NSA_N_BRANCH + br, 0, 0))
            nsel_spec = pl.BlockSpec((1, NSA_KV_HEADS, tq, LANES), lambda bi, ci, qi: (bi, 0, qi, 0))
            eneg_spec = pl.BlockSpec(eneg.shape, lambda bi, ci, qi: (0, 0))
            o_cs = _flash("sel", q3, 4, kt3, 4, vv3, 4, NSA_HPG, False, F32,
                          extras=((notsel, nsel_spec), (eneg, eneg_spec), (gate3, gate_spec), (r_tab, r_spec(1)),
                                  (o_c, prev_spec)), tq=tq)
            o_b = _flash("win", q3, 4, kt3, 5, vv3, 5, NSA_HPG, False, BF16,
                         extras=((gate3, gate_spec), (r_tab, r_spec(2)), (o_cs, prev_spec)),
                         window=NSA_WINDOW, tq=tq)
            wo_a, wo_b = _even_out_weight(w_out_even[li])
            xf = _outproj_ln([o_a.reshape(n, -1), o_b.reshape(n, -1)], [wo_a, wo_b], xf,
                             row(ln_gain[layer, 0]), row(ln_bias[layer, 0]))
        else:
            q, k, v, fl = _proj(xf, _odd_in_weight(w_in_odd[li]), None, ODD_SEGS, ODD_DTYPES, s)
            bias = jnp.pad(fox_f_bias[li].astype(F32), (0, LANES - FOX_HEADS)).reshape(1, LANES)
            c_col, c_pieces = _fox_cumsum(fl.reshape(b, s, LANES), bias)
            caug = _fox_key_aug(c_pieces)
            ccol_spec = pl.BlockSpec((1, tq, LANES), lambda bi, ci, qi: (bi, qi, 0))
            caug_spec = pl.BlockSpec((1, 1, FOX_AUG_ROWS, s), lambda bi, ci, qi: (bi, ci, 0, 0))
            o_f = _flash("fox", q.reshape(b, s, -1), 0, t3(k), 0, v.reshape(b, s, -1), 0, FOX_HEADS // 2, True, BF16,
                         extras=((c_col, ccol_spec), (caug, caug_spec)), tq=tq)
            xf = _outproj_ln([o_f.reshape(n, -1)], [w_out_odd[li].astype(BF16)], xf,
                             row(ln_gain[layer, 0]), row(ln_bias[layer, 0]))
        xf = _mlp_ln(xf, mlp_w_up[layer].astype(BF16), mlp_w_down[layer].astype(BF16),
                     row(ln_gain[layer, 1]), row(ln_bias[layer, 1]))
    return xf.reshape(b, s, d)
```

```python
import functools
import math

import numpy as np
import jax
import jax.numpy as jnp
from jax import lax
from jax.experimental import pallas as pl
from jax.experimental.pallas import tpu as pltpu

F32 = jnp.float32
BF16 = jnp.bfloat16

D_MODEL = 1024
DEPTH = 4
HEAD_DIM = 64
ROPE_DIM = HEAD_DIM // 4
ROPE_THETA = 500000.0
LN_EPS = 1e-5
DIFF_HEADS = 4
NSA_HEADS = 8
NSA_KV_HEADS = 2
NSA_HPG = NSA_HEADS // NSA_KV_HEADS
NSA_CMP_LEN = 32
NSA_CMP_STRIDE = 16
NSA_CMP_HIDDEN = 256
NSA_SEL_LEN = 64
NSA_TOPK = 8
NSA_WINDOW = 512
NSA_N_BRANCH = 3
NSA_BIG = 1e4
FOX_HEADS = D_MODEL // HEAD_DIM
D_FF = 4 * D_MODEL
ALPHA = (2 * DEPTH) ** 0.25

LANES = 128
NEG = -1e30
VMEM_LIMIT = 56 * 1024 * 1024
LOG2E = 1.4426950408889634
MASK_BIG = 30000.0
FOX_NEG_LANE = 3 * 16
FOX_AUG_ROWS = HEAD_DIM
FLASH_TQ = 2048

EVEN_SEGS = ((0, 1024, True), (1024, 768, True), (1792, 768, False),
             (2560, 128, False), (2688, 128, False), (2816, 128, False))
EVEN_DTYPES = (BF16, BF16, BF16, BF16, BF16, F32)
ODD_SEGS = ((0, D_MODEL, False), (D_MODEL, D_MODEL, False), (2 * D_MODEL, D_MODEL, False),
            (3 * D_MODEL, 128, False))
ODD_DTYPES = (BF16, BF16, BF16, F32)


def _dot(a, b):
    return jnp.dot(a, b, preferred_element_type=F32)


def _dot_nt(a, b):
    return lax.dot_general(a, b, (((1,), (1,)), ((), ())), preferred_element_type=F32)


def _rep_lanes(x, n):
    return x if n == 1 else jnp.concatenate([x] * n, axis=1)


def _split3(x):
    p0 = x.astype(BF16)
    r1 = x - p0.astype(F32)
    p1 = r1.astype(BF16)
    p2 = (r1 - p1.astype(F32)).astype(BF16)
    return p0, p1, p2


def _rope(h, cos, sa, sb):
    return h * cos + pltpu.roll(h, LANES - ROPE_DIM // 2, 1) * sa + pltpu.roll(h, ROPE_DIM // 2, 1) * sb


def _layer_norm(y, g, b):
    mu = jnp.mean(y, axis=-1, keepdims=True)
    yc = y - mu
    var = jnp.mean(yc * yc, axis=-1, keepdims=True)
    return yc * lax.rsqrt(var + LN_EPS) * g + b


def _params(*sem):
    return pltpu.CompilerParams(dimension_semantics=sem, vmem_limit_bytes=VMEM_LIMIT)


def _proj_kernel(*refs, segs, has_rope):
    x_ref, w_ref = refs[0], refs[1]
    pos = 2
    if has_rope:
        cos_ref, sa_ref, sb_ref = refs[2:5]
        pos = 5
    out_refs = refs[pos:]
    xb = x_ref[...].astype(BF16)
    for o_ref, (c0, width, rope) in zip(out_refs, segs):
        for cc in range(0, width, 512):
            w = min(512, width - cc)
            h = _dot(xb, w_ref[:, c0 + cc:c0 + cc + w])
            if rope:
                cos, sa, sb = cos_ref[...], sa_ref[...], sb_ref[...]
                for t in range(0, w, LANES):
                    o_ref[:, cc + t:cc + t + LANES] = _rope(h[:, t:t + LANES], cos, sa, sb).astype(o_ref.dtype)
            else:
                o_ref[:, cc:cc + w] = h.astype(o_ref.dtype)


def _proj(x2d, w, tables, segs, dtypes, seq, tm=512):
    n, d = x2d.shape
    m = w.shape[1]
    ns = seq // tm
    has_rope = tables is not None
    in_specs = [pl.BlockSpec((tm, d), lambda i: (i, 0)), pl.BlockSpec((d, m), lambda i: (0, 0))]
    args = [x2d, w]
    if has_rope:
        in_specs += [pl.BlockSpec((tm, LANES), lambda i: (i % ns, 0))] * 3
        args += list(tables)
    return pl.pallas_call(
        functools.partial(_proj_kernel, segs=segs, has_rope=has_rope),
        grid=(n // tm,),
        in_specs=in_specs,
        out_specs=[pl.BlockSpec((tm, s[1]), lambda i: (i, 0)) for s in segs],
        out_shape=[jax.ShapeDtypeStruct((n, s[1]), dt) for s, dt in zip(segs, dtypes)],
        compiler_params=_params("parallel"),
        name="in_proj",
    )(*args)


def _outproj_ln_kernel(*refs, n_in):
    o_refs = refs[:n_in]
    w_refs = refs[n_in:2 * n_in]
    x_ref, g_ref, b_ref, y_ref = refs[2 * n_in:]
    mix = _dot(o_refs[0][...], w_refs[0][...])
    for o_ref, w_ref in zip(o_refs[1:], w_refs[1:]):
        mix = mix + _dot(o_ref[...], w_ref[...])
    y_ref[...] = _layer_norm(ALPHA * x_ref[...] + mix, g_ref[...], b_ref[...])


def _outproj_ln(os_, ws, x2d, g, b, tm=512):
    n, d = x2d.shape
    in_specs = [pl.BlockSpec((tm, o.shape[1]), lambda i: (i, 0)) for o in os_]
    in_specs += [pl.BlockSpec(w.shape, lambda i: (0, 0)) for w in ws]
    in_specs += [pl.BlockSpec((tm, d), lambda i: (i, 0)),
                 pl.BlockSpec((1, d), lambda i: (0, 0)), pl.BlockSpec((1, d), lambda i: (0, 0))]
    return pl.pallas_call(
        functools.partial(_outproj_ln_kernel, n_in=len(os_)),
        grid=(n // tm,),
        in_specs=in_specs,
        out_specs=pl.BlockSpec((tm, d), lambda i: (i, 0)),
        out_shape=jax.ShapeDtypeStruct((n, d), F32),
        compiler_params=_params("parallel"),
        name="out_proj_ln",
    )(*os_, *ws, x2d, g, b)


def _mlp_ln_kernel(x_ref, wu_ref, wd_ref, g_ref, b_ref, y_ref, *, ff_chunk):
    x = x_ref[...]
    xb = x.astype(BF16)
    acc = None
    for c0 in range(0, wu_ref.shape[1], ff_chunk):
        h = jnp.maximum(_dot(xb, wu_ref[:, c0:c0 + ff_chunk]), 0.0)
        part = _dot((h * h).astype(BF16), wd_ref[c0:c0 + ff_chunk, :])
        acc = part if acc is None else acc + part
    y_ref[...] = _layer_norm(ALPHA * x + acc, g_ref[...], b_ref[...])


def _mlp_ln(x2d, wu, wd, g, b, tm=512, ff_chunk=1024):
    n, d = x2d.shape
    resident = pl.Buffered(1)
    return pl.pallas_call(
        functools.partial(_mlp_ln_kernel, ff_chunk=ff_chunk),
        grid=(n // tm,),
        in_specs=[pl.BlockSpec((tm, d), lambda i: (i, 0)),
                  pl.BlockSpec(wu.shape, lambda i: (0, 0), pipeline_mode=resident),
                  pl.BlockSpec(wd.shape, lambda i: (0, 0), pipeline_mode=resident),
                  pl.BlockSpec((1, d), lambda i: (0, 0)), pl.BlockSpec((1, d), lambda i: (0, 0))],
        out_specs=pl.BlockSpec((tm, d), lambda i: (i, 0)),
        out_shape=jax.ShapeDtypeStruct((n, d), F32),
        compiler_params=_params("parallel"),
        name="mlp_ln",
    )(x2d, wu, wd, g, b)


def _fox_cumsum_kernel(f_ref, bias_ref, tri_ref, scat_ref, negrow_ref, cp_ref, qaug_ref, *, blk):
    tri = tri_ref[...]
    carry = jnp.zeros((1, LANES), F32)
    for r0 in range(0, f_ref.shape[1], blk):
        z = f_ref[0, r0:r0 + blk, :] + bias_ref[...]
        lf = jnp.minimum(z, 0.0) - jnp.log1p(jnp.exp(-jnp.abs(z)))
        p0, p1, p2 = _split3(lf)
        cs = _dot(tri, p0) + _dot(tri, p1) + _dot(tri, p2) + carry
        pieces = _split3(cs * LOG2E)
        qaug = negrow_ref[...]
        for i, piece in enumerate(pieces):
            cp_ref[i, 0, r0:r0 + blk, :] = piece
            qaug = qaug + _dot(piece, scat_ref[i])
        qaug_ref[0, r0:r0 + blk, :] = qaug.astype(BF16)
        carry = cs[blk - 1:blk, :]


def _fox_cumsum(f3d, bias, blk=256):
    b, s, _ = f3d.shape
    blk = min(blk, s)
    tri = jnp.asarray(np.tril(np.ones((blk, blk), np.float32)), BF16)
    scat = np.zeros((3, LANES, LANES), np.float32)
    negrow = np.zeros((1, LANES), np.float32)
    for half in (0, HEAD_DIM):
        for head in range(FOX_HEADS):
            for i in range(3):
                scat[i, head, half + 3 * head + i] = 1.0
        negrow[0, half + FOX_NEG_LANE:half + FOX_NEG_LANE + 3] = -1.0
    return pl.pallas_call(
        functools.partial(_fox_cumsum_kernel, blk=blk),
        grid=(b,),
        in_specs=[pl.BlockSpec((1, s, LANES), lambda i: (i, 0, 0)),
                  pl.BlockSpec((1, LANES), lambda i: (0, 0)),
                  pl.BlockSpec((blk, blk), lambda i: (0, 0)),
                  pl.BlockSpec((3, LANES, LANES), lambda i: (0, 0, 0)),
                  pl.BlockSpec((1, LANES), lambda i: (0, 0))],
        out_specs=[pl.BlockSpec((3, 1, s, LANES), lambda i: (0, i, 0, 0)),
                   pl.BlockSpec((1, s, LANES), lambda i: (i, 0, 0))],
        out_shape=[jax.ShapeDtypeStruct((3, b, s, LANES), BF16),
                   jax.ShapeDtypeStruct((b, s, LANES), BF16)],
        compiler_params=_params("parallel"),
        name="fox_cumsum",
    )(f3d, bias, tri, jnp.asarray(scat, BF16), jnp.asarray(negrow))


def _flash_kernel(*refs, mode, tq, tk, rc, nq, window, lam_init):
    it = iter(refs)
    q_ref, kt_ref, v_ref = next(it), next(it), next(it)
    if mode == "fox":
        qaug_ref, caug_ref = next(it), next(it)
    if mode == "diff":
        lam_ref, subln_ref = next(it), next(it)
    if mode == "sel":
        nsel_ref, eneg_ref = next(it), next(it)
    if mode in ("sel", "win"):
        gate_ref, r_ref, prev_ref = next(it), next(it), next(it)
    o_ref, qa_ref, m_ref, acc_ref = next(it), next(it), next(it), next(it)

    c = pl.program_id(1)
    qi = 0 if nq == 1 else pl.program_id(2)
    acc_w = acc_ref.shape[1]
    q = q_ref[0]
    lane = lax.broadcasted_iota(jnp.int32, (tq, LANES), 1)
    lo_half = lane < HEAD_DIM
    if mode == "fox":
        qaug = qaug_ref[0]
    elif mode == "sel":
        qaug = nsel_ref[0]
    else:
        qaug = jnp.zeros_like(q)
    qa_ref[0:tq, :] = jnp.where(lo_half, q, qaug)
    qa_ref[tq:2 * tq, :] = jnp.where(lo_half, qaug, q)
    m_ref[...] = jnp.full(m_ref.shape, NEG, F32)
    acc_ref[...] = jnp.zeros(acc_ref.shape, F32)

    lane_k = lax.broadcasted_iota(jnp.int32, (tk, LANES), 1)
    lo_half_k = lane_k < HEAD_DIM

    nsub = tq // tk

    def tile(kt, d, chunks):
        k0 = kt * tk if isinstance(kt, int) else pl.multiple_of(kt * tk, tk)
        kb = kt_ref[0, :, pl.ds(k0, tk)]
        if mode in ("fox", "sel"):
            aug = ((caug_ref[0, 0, :, pl.ds(k0, tk)], caug_ref[0, 1, :, pl.ds(k0, tk)]) if mode == "fox"
                   else (eneg_ref[:, pl.ds(k0, tk)],) * 2)
            kbs = (jnp.concatenate([kb[0:HEAD_DIM], aug[0]], axis=0),
                   jnp.concatenate([aug[1], kb[HEAD_DIM:LANES]], axis=0))
        else:
            kbs = (kb, kb)
        vb = v_ref[0, pl.ds(k0, tk), :]
        one = jnp.ones_like(vb)
        if mode == "diff":
            vaug = (jnp.concatenate([vb, one], axis=1),) * 2
        else:
            vaug = (jnp.where(lo_half_k, vb, one), jnp.where(lo_half_k, one, vb))
        for h in (0, 1):
            for rr, causal, win in chunks:
                r0 = h * tq + rr
                s = _dot(qa_ref[r0:r0 + rc, :], kbs[h])
                if causal or win:
                    rows = lax.broadcasted_iota(jnp.int32, (rc, tk), 0)
                    cols = lax.broadcasted_iota(jnp.int32, (rc, tk), 1)
                    dist = (rr - d * tk) + rows - cols
                    if causal:
                        s = jnp.where(dist >= 0, s, NEG)
                    if win:
                        s = jnp.where(dist < window, s, NEG)
                m_prev = m_ref[r0:r0 + rc, :]
                m_new = jnp.maximum(m_prev, jnp.max(s, axis=-1, keepdims=True))
                p = jnp.exp2((s - _rep_lanes(m_new, tk // LANES)).astype(BF16))
                alpha = jnp.exp2(m_prev - m_new)
                acc_ref[r0:r0 + rc, :] = (_rep_lanes(alpha, acc_w // LANES) * acc_ref[r0:r0 + rc, :]
                                          + _dot(p, vaug[h]))
                m_ref[r0:r0 + rc, :] = m_new

    all_rows = range(0, tq, rc)
    if mode == "win":
        n_back = window // tk
        for d in range(-n_back, nsub):
            chunks = [(rr, rr // tk == d, rr // tk == d + n_back) for rr in all_rows if d <= rr // tk <= d + n_back]
            kt = qi * nsub + d
            if isinstance(kt, int):
                if kt >= 0:
                    tile(kt, d, chunks)
            elif d < 0:
                pl.when(kt >= 0)(functools.partial(tile, kt, d, chunks))
            else:
                tile(kt, d, chunks)
    else:
        def body(kt, carry):
            tile(kt, 0, [(rr, False, False) for rr in all_rows])
            return carry
        lax.fori_loop(0, qi * nsub, body, 0)
        for d in range(nsub):
            tile(qi * nsub + d, d, [(rr, rr // tk == d, False) for rr in all_rows if rr // tk >= d])

    if mode == "diff":
        lp = lam_ref[...]
        lam = (jnp.exp(jnp.sum(lp[0:1] * lp[1:2], axis=-1, keepdims=True))
               - jnp.exp(jnp.sum(lp[2:3] * lp[3:4], axis=-1, keepdims=True)) + lam_init)
        o = (acc_ref[0:tq, 0:LANES] / acc_ref[0:tq, LANES:2 * LANES]
             - lam * (acc_ref[tq:2 * tq, 0:LANES] / acc_ref[tq:2 * tq, LANES:2 * LANES]))
        ms = jnp.mean(o * o, axis=-1, keepdims=True)
        o = o * lax.rsqrt(ms + LN_EPS) * subln_ref[...] * (1.0 - lam_init)
    else:
        acc0, acc1 = acc_ref[0:tq, :], acc_ref[tq:2 * tq, :]
        o0, o1 = acc0 / pltpu.roll(acc0, HEAD_DIM, 1), acc1 / pltpu.roll(acc1, HEAD_DIM, 1)
        o = jnp.where(lo_half, o0, o1)
        if mode == "fox":
            ii = lax.broadcasted_iota(jnp.int32, (LANES, LANES), 0)
            esel = jnp.where(ii == FOX_NEG_LANE, -1.0, 0.0).astype(BF16)
            x = qaug_ref[0].astype(F32)
            hi = x.astype(BF16)
            lo = (x - hi.astype(F32)).astype(BF16)
            o = (_dot(hi, esel) + _dot(lo, esel)) * o
    if mode in ("sel", "win"):
        o = prev_ref[0] + _gate_lanes(gate_ref[0], r_ref[0]) * o
    o_ref[0] = o.astype(o_ref.dtype)


def _gate_lanes(gate_logits, r):
    sig = jax.nn.sigmoid(gate_logits)
    hi = sig.astype(BF16)
    lo = (sig - hi.astype(F32)).astype(BF16)
    return _dot(hi, r) + _dot(lo, r)


def _flash(mode, q_arr, q_off, kt_arr, k_off, v_arr, v_off, n_chunks, kv_per_chunk, out_dtype,
           extras=(), window=0, lam_init=0.0, tq=256, tk=256, rc=256):
    b, s, _ = q_arr.shape
    tq = min(tq, s)
    tk = min(tk, tq)
    rc = min(rc, tk)
    nq = s // tq
    kt_map = (lambda bi, ci, qi: (bi, k_off + ci, 0)) if kv_per_chunk else (lambda bi, ci, qi: (bi, k_off, 0))
    v_map = (lambda bi, ci, qi: (bi, 0, v_off + ci)) if kv_per_chunk else (lambda bi, ci, qi: (bi, 0, v_off))
    in_specs = [pl.BlockSpec((1, tq, LANES), lambda bi, ci, qi: (bi, qi, q_off + ci)),
                pl.BlockSpec((1, LANES, s), kt_map),
                pl.BlockSpec((1, s, LANES), v_map)]
    args = [q_arr, kt_arr, v_arr]
    for arr, spec in extras:
        args.append(arr)
        in_specs.append(spec)
    acc_w = 2 * LANES if mode == "diff" else LANES
    return pl.pallas_call(
        functools.partial(_flash_kernel, mode=mode, tq=tq, tk=tk, rc=rc, nq=nq, window=window, lam_init=lam_init),
        grid=(b, n_chunks, nq),
        in_specs=in_specs,
        out_specs=pl.BlockSpec((1, tq, LANES), lambda bi, ci, qi: (bi, qi, ci)),
        out_shape=jax.ShapeDtypeStruct((b, s, n_chunks * LANES), out_dtype),
        scratch_shapes=[pltpu.VMEM((2 * tq, LANES), BF16), pltpu.VMEM((2 * tq, LANES), F32),
                        pltpu.VMEM((2 * tq, acc_w), F32)],
        compiler_params=_params("parallel", "parallel", "arbitrary"),
        name="flash_" + mode,
    )(*args)


def _nsa_compress_kernel(tk_ref, tv_ref, w1a_ref, w1b_ref, pe_ref, b1_ref, w2_ref,
                         cos_ref, sa_ref, sb_ref, kc_ref, vc_ref):
    ncp = tk_ref.shape[1]
    for j, (t_ref, o_ref) in enumerate(((tk_ref, kc_ref), (tv_ref, vc_ref))):
        t = t_ref[0]
        u = _dot(t, w1a_ref[j])
        v = _dot(t, w1b_ref[j])
        const = _dot(pe_ref[j, 0], w1a_ref[j]) + _dot(pe_ref[j, 1], w1b_ref[j])
        pre = u + pltpu.roll(v, ncp - 1, 0) + const[0:1] + b1_ref[j]
        h = jax.nn.gelu(pre, approximate=True)
        out = _dot(h.astype(BF16), w2_ref[j])
        if j == 0:
            out = _rope(out, cos_ref[...], sa_ref[...], sb_ref[...])
        o_ref[0] = out.astype(o_ref.dtype)


def _nsa_compress(tkc, tvc, w1a, w1b, pe, b1, w2, tables_c):
    b, ncp, width = tkc.shape
    full = lambda a: pl.BlockSpec(a.shape, lambda i: (0,) * a.ndim)
    return pl.pallas_call(
        _nsa_compress_kernel,
        grid=(b,),
        in_specs=[pl.BlockSpec((1, ncp, width), lambda i: (i, 0, 0)),
                  pl.BlockSpec((1, ncp, width), lambda i: (i, 0, 0)),
                  full(w1a), full(w1b), full(pe), full(b1), full(w2)] + [full(t) for t in tables_c],
        out_specs=[pl.BlockSpec((1, ncp, LANES), lambda i: (i, 0, 0))] * 2,
        out_shape=[jax.ShapeDtypeStruct((b, ncp, LANES), BF16)] * 2,
        compiler_params=_params("parallel"),
        name="nsa_compress",
    )(tkc, tvc, w1a, w1b, pe, b1, w2, *tables_c)


def _nsa_cmp_kernel(q_ref, kc_ref, vc_ref, gate_ref, r_ref, ovt_ref, eye_ref, oc_ref, nsel_ref, *, tq, nsel, topk):
    qi = pl.program_id(1)
    kc = kc_ref[0]
    vc = vc_ref[0]
    ncp = kc.shape[0]
    ovt = ovt_ref[...]
    nsr = -(-nsel // 8) * 8
    lane = lax.broadcasted_iota(jnp.int32, (tq, LANES), 1)
    lo_half = lane < HEAD_DIM
    tpos_c = qi * tq + lax.broadcasted_iota(jnp.int32, (tq, ncp), 0)
    ncol = lax.broadcasted_iota(jnp.int32, (tq, ncp), 1)
    valid = (NSA_CMP_STRIDE * ncol + NSA_CMP_LEN - 1) <= tpos_c
    blk = lax.broadcasted_iota(jnp.int32, (nsr, tq), 0)
    cur = (qi * tq + lax.broadcasted_iota(jnp.int32, (nsr, tq), 1)) // NSA_SEL_LEN
    gates = jax.nn.sigmoid(gate_ref[0])
    g_hi = gates.astype(BF16)
    g_lo = (gates - g_hi.astype(F32)).astype(BF16)

    o_chunks = [None] * NSA_HPG
    notsel_t = [None] * NSA_KV_HEADS
    for g in range(NSA_KV_HEADS):
        in_half = lo_half if g == 0 else jnp.logical_not(lo_half)
        psum = jnp.zeros((tq, ncp), F32)
        for j in range(NSA_HPG):
            qc = q_ref[0, :, j * LANES:(j + 1) * LANES]
            s = _dot_nt(jnp.where(in_half, qc, jnp.zeros_like(qc)), kc)
            s = jnp.where(valid, s, NEG)
            e = jnp.exp2(s - jnp.max(s, axis=-1, keepdims=True))
            p = jnp.where(valid, e / jnp.sum(e, axis=-1, keepdims=True), 0.0)
            psum = psum + p
            o = _dot(p.astype(BF16), vc)
            o_chunks[j] = o if g == 0 else jnp.where(lo_half, o_chunks[j], o)
        p_hi = psum.astype(BF16)
        p_lo = (psum - p_hi.astype(F32)).astype(BF16)
        imp = (_dot_nt(ovt, p_hi) + _dot_nt(ovt, p_lo))[0:nsr]
        imp = jnp.where(blk > cur, -NSA_BIG, imp)
        imp = jnp.where(blk == cur - 1, NSA_BIG, imp)
        imp = jnp.where(blk == cur, NSA_BIG, imp)
        imp = jnp.where(blk == 0, NSA_BIG, imp)
        imp = jnp.where(blk >= nsel, -3.0 * NSA_BIG, imp)
        rank = jnp.zeros((nsr, tq), jnp.int32)
        for jp in range(nsel):
            other = imp[jp:jp + 1, :]
            ahead = jnp.where(other > imp, 1, jnp.where(other == imp, jnp.where(blk > jp, 1, 0), 0))
            rank = rank + ahead
        notsel_t[g] = jnp.where(rank < topk, 0.0, 1.0)
    gap = jnp.zeros((HEAD_DIM - nsr, tq), F32)
    both_t = jnp.concatenate([notsel_t[1], gap, notsel_t[0], gap], axis=0).astype(BF16)
    nsel_ref[0] = _dot_nt(eye_ref[...], both_t).astype(nsel_ref.dtype)
    for j in range(NSA_HPG):
        r = r_ref[j * NSA_N_BRANCH]
        oc_ref[0, :, j * LANES:(j + 1) * LANES] = (_dot(g_hi, r) + _dot(g_lo, r)) * o_chunks[j]


def _nsa_cmp(q3d, q_blk, kc, vc, gate3d, r_tab, ovt, nsel, tq=256):
    b, s, _ = q3d.shape
    tq = min(tq, s)
    ncp = kc.shape[1]
    width = NSA_HPG * LANES
    eye = jnp.asarray(np.eye(tq, dtype=np.float32), BF16)
    full = lambda a: pl.BlockSpec(a.shape, lambda bi, qi: (0,) * a.ndim)
    return pl.pallas_call(
        functools.partial(_nsa_cmp_kernel, tq=tq, nsel=nsel, topk=min(NSA_TOPK, nsel)),
        grid=(b, s // tq),
        in_specs=[pl.BlockSpec((1, tq, width), lambda bi, qi: (bi, qi, q_blk)),
                  pl.BlockSpec((1, ncp, LANES), lambda bi, qi: (bi, 0, 0)),
                  pl.BlockSpec((1, ncp, LANES), lambda bi, qi: (bi, 0, 0)),
                  pl.BlockSpec((1, tq, LANES), lambda bi, qi: (bi, qi, 0)),
                  full(r_tab), full(ovt), full(eye)],
        out_specs=[pl.BlockSpec((1, tq, width), lambda bi, qi: (bi, qi, 0)),
                   pl.BlockSpec((1, tq, LANES), lambda bi, qi: (bi, qi, 0))],
        out_shape=[jax.ShapeDtypeStruct((b, s, width), F32),
                   jax.ShapeDtypeStruct((b, s, LANES), BF16)],
        compiler_params=_params("parallel", "arbitrary"),
        name="nsa_cmp_topk",
    )(q3d, kc, vc, gate3d, r_tab, ovt, eye)


def _rope_tables(pos):
    half = ROPE_DIM // 2
    inv = ROPE_THETA ** (-jnp.arange(half, dtype=F32) / half)
    ang = pos.astype(F32)[:, None] * inv[None, :]
    cos, sin = jnp.cos(ang), jnp.sin(ang)
    ones = jnp.ones((pos.shape[0], HEAD_DIM - ROPE_DIM), F32)
    zeros = jnp.zeros((pos.shape[0], HEAD_DIM - ROPE_DIM), F32)
    zh = jnp.zeros_like(sin)
    c64 = jnp.concatenate([cos, cos, ones], axis=1)
    sa64 = jnp.concatenate([-sin, zh, zeros], axis=1)
    sb64 = jnp.concatenate([zh, sin, zeros], axis=1)
    return tuple(jnp.concatenate([t, t], axis=1) for t in (c64, sa64, sb64))


def _even_in_weight(w):
    o = np.cumsum([0, 512, 512, 512, 512, 128, 128, 128, 128, 128, 128, 24])
    qa, ka, va, qn, kc, vc, ksl, vsl, kw, vw, g = (np.arange(o[i], o[i + 1]) for i in range(11))
    qn_perm = np.concatenate([qn[(gi * NSA_HPG + j) * HEAD_DIM:(gi * NSA_HPG + j + 1) * HEAD_DIM]
                              for j in range(NSA_HPG) for gi in range(NSA_KV_HEADS)])
    cols = np.concatenate([qa, qn_perm, ka, ksl, kw, va, vsl, vw, kc, vc, g])
    scale = np.ones(cols.shape[0], np.float32)
    scale[0:1024] = HEAD_DIM ** -0.5 * LOG2E
    wp = jnp.take(w, jnp.asarray(cols), axis=1) * jnp.asarray(scale)[None, :]
    wp = jnp.pad(wp, ((0, 0), (0, LANES - g.shape[0])))
    return wp.astype(BF16)


def _even_out_weight(w):
    rows_b = np.concatenate([512 + np.arange((gi * NSA_HPG + j) * HEAD_DIM, (gi * NSA_HPG + j + 1) * HEAD_DIM)
                             for j in range(NSA_HPG) for gi in range(NSA_KV_HEADS)])
    return w[:512].astype(BF16), jnp.take(w, jnp.asarray(rows_b), axis=0).astype(BF16)


def _odd_in_weight(w):
    scale = np.ones(w.shape[1], np.float32)
    scale[:D_MODEL] = HEAD_DIM ** -0.5 * LOG2E
    wp = w * jnp.asarray(scale)[None, :]
    return jnp.pad(wp, ((0, 0), (0, LANES - FOX_HEADS))).astype(BF16)


def _compress_weights(pe, w1, b1, w2):
    eye = jnp.eye(NSA_KV_HEADS, dtype=F32)
    w1r = w1.reshape(2, 2, NSA_CMP_STRIDE, HEAD_DIM, NSA_CMP_HIDDEN)
    w1x = jnp.einsum("jardc,gh->jargdhc", w1r, eye)
    w1x = w1x.reshape(2, 2, NSA_CMP_STRIDE * LANES, NSA_KV_HEADS * NSA_CMP_HIDDEN)
    pe_r = pe.reshape(2, 2, NSA_CMP_STRIDE, 1, HEAD_DIM)
    pe_x = jnp.broadcast_to(pe_r, (2, 2, NSA_CMP_STRIDE, NSA_KV_HEADS, HEAD_DIM)).reshape(2, 2, 1, NSA_CMP_STRIDE * LANES)
    pe_x = jnp.broadcast_to(pe_x, (2, 2, 8, NSA_CMP_STRIDE * LANES))
    b1x = jnp.tile(b1, (1, NSA_KV_HEADS)).reshape(2, 1, NSA_KV_HEADS * NSA_CMP_HIDDEN)
    w2x = jnp.einsum("jcd,gh->jgchd", w2, eye).reshape(2, NSA_KV_HEADS * NSA_CMP_HIDDEN, LANES)
    return (w1x[:, 0].astype(BF16), w1x[:, 1].astype(BF16), pe_x.astype(BF16), b1x.astype(F32), w2x.astype(BF16))


def _nsa_constants(seq):
    ncp = seq // NSA_CMP_STRIDE
    nsel = seq // NSA_SEL_LEN
    n = np.arange(ncp)[:, None]
    j = np.arange(LANES)[None, :]
    cs, ce = n * NSA_CMP_STRIDE, n * NSA_CMP_STRIDE + NSA_CMP_LEN - 1
    ss = j * NSA_SEL_LEN
    ov = ((cs <= ss + NSA_SEL_LEN - 1) & (ce >= ss) & (j < nsel) & (n < ncp - 1)).astype(np.float32)
    tok = np.arange(seq)
    eneg = -MASK_BIG * (tok[None, :] // NSA_SEL_LEN == np.arange(HEAD_DIM)[:, None]).astype(np.float32)
    r = np.zeros((NSA_HPG * NSA_N_BRANCH, LANES, LANES), np.float32)
    for jc in range(NSA_HPG):
        for br in range(NSA_N_BRANCH):
            for ln in range(LANES):
                head = (ln // HEAD_DIM) * NSA_HPG + jc
                r[jc * NSA_N_BRANCH + br, head * NSA_N_BRANCH + br, ln] = 1.0
    return jnp.asarray(ov.T.copy(), BF16), jnp.asarray(eneg, BF16), jnp.asarray(r, BF16), nsel


def _fox_key_aug(c_pieces):
    _, b, s, _ = c_pieces.shape
    pieces = jnp.transpose(c_pieces[..., :FOX_HEADS], (1, 3, 0, 2))
    own = (np.arange(FOX_NEG_LANE)[None, :] // 3 == np.arange(FOX_HEADS)[:, None]).astype(np.float32)
    ones = jnp.broadcast_to(jnp.asarray(own, BF16)[None, :, :, None], (b, FOX_HEADS, FOX_NEG_LANE, s))
    pad = jnp.zeros((b, FOX_HEADS, FOX_AUG_ROWS - FOX_NEG_LANE - 3, s), BF16)
    return jnp.concatenate([ones, pieces, pad], axis=2)


def kernel(x, ln_gain, ln_bias, mlp_w_up, mlp_w_down, w_in_even, w_out_even, diff_lambda, diff_subln,
           nsa_pe, nsa_cmp_w1, nsa_cmp_b1, nsa_cmp_w2, w_in_odd, fox_f_bias, w_out_odd):
    b, s, d = x.shape
    n = b * s
    tq = min(FLASH_TQ, s)
    ncp = s // NSA_CMP_STRIDE
    xf = x.reshape(n, d)
    tables = _rope_tables(jnp.arange(s))
    tables_c = _rope_tables(jnp.arange(ncp) * NSA_CMP_STRIDE + NSA_CMP_LEN - 1)
    ov, eneg, r_tab, nsel = _nsa_constants(s)
    row = lambda a: a.reshape(1, -1).astype(F32)
    t3 = lambda a: jnp.transpose(a.reshape(b, s, -1), (0, 2, 1))

    for layer in range(DEPTH):
        li = layer // 2
        if layer % 2 == 0:
            q, k, vv, kci, vci, gate = _proj(xf, _even_in_weight(w_in_even[li]), tables, EVEN_SEGS, EVEN_DTYPES, s)
            q3, kt3, vv3, gate3 = q.reshape(b, s, -1), t3(k), vv.reshape(b, s, -1), gate.reshape(b, s, LANES)
            lam_init = 0.8 - 0.6 * math.exp(-0.3 * layer)
            full2 = lambda a: pl.BlockSpec(a.shape, lambda bi, ci, qi: (0, 0))
            lam_p, subln = diff_lambda[li].astype(F32), row(diff_subln[li])
            o_a = _flash("diff", q3, 0, kt3, 0, vv3, 0, DIFF_HEADS, True, BF16,
                         extras=((lam_p, full2(lam_p)), (subln, full2(subln))), lam_init=lam_init, tq=tq)
            w1a, w1b, pe_x, b1x, w2x = _compress_weights(nsa_pe[li], nsa_cmp_w1[li], nsa_cmp_b1[li], nsa_cmp_w2[li])
            kc, vc = _nsa_compress(kci.reshape(b, ncp, -1), vci.reshape(b, ncp, -1), w1a, w1b, pe_x, b1x, w2x, tables_c)
            o_c, notsel = _nsa_cmp(q3, 1, kc, vc, gate3, r_tab, ov, nsel)
            gate_spec = pl.BlockSpec((1, tq, LANES), lambda bi, ci, qi: (bi, qi, 0))
            prev_spec = pl.BlockSpec((1, tq, LANES), lambda bi, ci, qi: (bi, qi, ci))
            r_spec = lambda br: pl.BlockSpec((1, LANES, LANES), lambda bi, ci, qi: (ci * NSA_N_BRANCH + br, 0, 0))
            nsel_spec = pl.BlockSpec((1, tq, LANES), lambda bi, ci, qi: (bi, qi, 0))
            eneg_spec = pl.BlockSpec(eneg.shape, lambda bi, ci, qi: (0, 0))
            o_cs = _flash("sel", q3, 4, kt3, 4, vv3, 4, NSA_HPG, False, F32,
                          extras=((notsel, nsel_spec), (eneg, eneg_spec), (gate3, gate_spec), (r_tab, r_spec(1)),
                                  (o_c, prev_spec)), tq=tq)
            o_b = _flash("win", q3, 4, kt3, 5, vv3, 5, NSA_HPG, False, BF16,
                         extras=((gate3, gate_spec), (r_tab, r_spec(2)), (o_cs, prev_spec)),
                         window=NSA_WINDOW, tq=tq)
            wo_a, wo_b = _even_out_weight(w_out_even[li])
            xf = _outproj_ln([o_a.reshape(n, -1), o_b.reshape(n, -1)], [wo_a, wo_b], xf,
                             row(ln_gain[layer, 0]), row(ln_bias[layer, 0]))
        else:
            q, k, v, fl = _proj(xf, _odd_in_weight(w_in_odd[li]), None, ODD_SEGS, ODD_DTYPES, s)
            bias = jnp.pad(fox_f_bias[li].astype(F32), (0, LANES - FOX_HEADS)).reshape(1, LANES)
            c_pieces, qaug = _fox_cumsum(fl.reshape(b, s, LANES), bias)
            caug = _fox_key_aug(c_pieces)
            qaug_spec = pl.BlockSpec((1, tq, LANES), lambda bi, ci, qi: (bi, qi, 0))
            caug_spec = pl.BlockSpec((1, 2, FOX_AUG_ROWS, s), lambda bi, ci, qi: (bi, ci, 0, 0))
            o_f = _flash("fox", q.reshape(b, s, -1), 0, t3(k), 0, v.reshape(b, s, -1), 0, FOX_HEADS // 2, True, BF16,
                         extras=((qaug, qaug_spec), (caug, caug_spec)), tq=tq)
            xf = _outproj_ln([o_f.reshape(n, -1)], [w_out_odd[li].astype(BF16)], xf,
                             row(ln_gain[layer, 0]), row(ln_bias[layer, 0]))
        xf = _mlp_ln(xf, mlp_w_up[layer].astype(BF16), mlp_w_down[layer].astype(BF16),
                     row(ln_gain[layer, 1]), row(ln_bias[layer, 1]))
    return xf.reshape(b, s, d)
```

```python
import functools
import math

import numpy as np
import jax
import jax.numpy as jnp
from jax import lax
from jax.experimental import pallas as pl
from jax.experimental.pallas import tpu as pltpu

F32 = jnp.float32
BF16 = jnp.bfloat16

D_MODEL = 1024
DEPTH = 4
HEAD_DIM = 64
ROPE_DIM = HEAD_DIM // 4
ROPE_THETA = 500000.0
LN_EPS = 1e-5
DIFF_HEADS = 4
NSA_HEADS = 8
NSA_KV_HEADS = 2
NSA_HPG = NSA_HEADS // NSA_KV_HEADS
NSA_CMP_LEN = 32
NSA_CMP_STRIDE = 16
NSA_CMP_HIDDEN = 256
NSA_SEL_LEN = 64
NSA_TOPK = 8
NSA_WINDOW = 512
NSA_N_BRANCH = 3
NSA_BIG = 1e4
FOX_HEADS = D_MODEL // HEAD_DIM
D_FF = 4 * D_MODEL
ALPHA = (2 * DEPTH) ** 0.25

LANES = 128
NEG = -1e30
VMEM_LIMIT = 56 * 1024 * 1024
LOG2E = 1.4426950408889634
MASK_BIG = 30000.0
FOX_ONE_LANE = 3 * 16
FOX_AUG_ROWS = HEAD_DIM
ONES_ROWS = 16
SCORE_LOOKAHEAD = 8

EVEN_SEGS = ((0, 1024, True), (1024, 768, True), (1792, 768, False),
             (2560, 128, False), (2688, 128, False), (2816, 128, False))
EVEN_DTYPES = (BF16, BF16, BF16, BF16, BF16, F32)
ODD_SEGS = ((0, D_MODEL, False), (D_MODEL, D_MODEL, False), (2 * D_MODEL, D_MODEL, False),
            (3 * D_MODEL, 128, False))
ODD_DTYPES = (BF16, BF16, BF16, F32)


def _dot(a, b):
    return jnp.dot(a, b, preferred_element_type=F32)


def _dot_nt(a, b):
    return lax.dot_general(a, b, (((1,), (1,)), ((), ())), preferred_element_type=F32)


def _split3(x):
    p0 = x.astype(BF16)
    r1 = x - p0.astype(F32)
    p1 = r1.astype(BF16)
    p2 = (r1 - p1.astype(F32)).astype(BF16)
    return p0, p1, p2


def _rope(h, cos, sa, sb):
    return h * cos + pltpu.roll(h, LANES - ROPE_DIM // 2, 1) * sa + pltpu.roll(h, ROPE_DIM // 2, 1) * sb


def _layer_norm(y, g, b):
    mu = jnp.mean(y, axis=-1, keepdims=True)
    yc = y - mu
    var = jnp.mean(yc * yc, axis=-1, keepdims=True)
    return yc * lax.rsqrt(var + LN_EPS) * g + b


def _params(*sem):
    return pltpu.CompilerParams(dimension_semantics=sem, vmem_limit_bytes=VMEM_LIMIT)


def _proj_kernel(*refs, segs, has_rope):
    x_ref, w_ref = refs[0], refs[1]
    pos = 2
    if has_rope:
        cos_ref, sa_ref, sb_ref = refs[2:5]
        pos = 5
    out_refs = refs[pos:]
    xb = x_ref[...].astype(BF16)
    for o_ref, (c0, width, rope) in zip(out_refs, segs):
        for cc in range(0, width, 512):
            w = min(512, width - cc)
            h = _dot(xb, w_ref[:, c0 + cc:c0 + cc + w])
            if rope:
                cos, sa, sb = cos_ref[...], sa_ref[...], sb_ref[...]
                for t in range(0, w, LANES):
                    o_ref[:, cc + t:cc + t + LANES] = _rope(h[:, t:t + LANES], cos, sa, sb).astype(o_ref.dtype)
            else:
                o_ref[:, cc:cc + w] = h.astype(o_ref.dtype)


def _proj(x2d, w, tables, segs, dtypes, seq, tm=512):
    n, d = x2d.shape
    m = w.shape[1]
    ns = seq // tm
    has_rope = tables is not None
    in_specs = [pl.BlockSpec((tm, d), lambda i: (i, 0)), pl.BlockSpec((d, m), lambda i: (0, 0))]
    args = [x2d, w]
    if has_rope:
        in_specs += [pl.BlockSpec((tm, LANES), lambda i: (i % ns, 0))] * 3
        args += list(tables)
    return pl.pallas_call(
        functools.partial(_proj_kernel, segs=segs, has_rope=has_rope),
        grid=(n // tm,),
        in_specs=in_specs,
        out_specs=[pl.BlockSpec((tm, s[1]), lambda i: (i, 0)) for s in segs],
        out_shape=[jax.ShapeDtypeStruct((n, s[1]), dt) for s, dt in zip(segs, dtypes)],
        compiler_params=_params("parallel"),
        name="in_proj",
    )(*args)


def _outproj_ln_kernel(*refs, n_in):
    o_refs = refs[:n_in]
    w_refs = refs[n_in:2 * n_in]
    x_ref, g_ref, b_ref, y_ref = refs[2 * n_in:]
    mix = _dot(o_refs[0][...], w_refs[0][...])
    for o_ref, w_ref in zip(o_refs[1:], w_refs[1:]):
        mix = mix + _dot(o_ref[...], w_ref[...])
    y_ref[...] = _layer_norm(ALPHA * x_ref[...] + mix, g_ref[...], b_ref[...])


def _outproj_ln(os_, ws, x2d, g, b, tm=512):
    n, d = x2d.shape
    in_specs = [pl.BlockSpec((tm, o.shape[1]), lambda i: (i, 0)) for o in os_]
    in_specs += [pl.BlockSpec(w.shape, lambda i: (0, 0)) for w in ws]
    in_specs += [pl.BlockSpec((tm, d), lambda i: (i, 0)),
                 pl.BlockSpec((1, d), lambda i: (0, 0)), pl.BlockSpec((1, d), lambda i: (0, 0))]
    return pl.pallas_call(
        functools.partial(_outproj_ln_kernel, n_in=len(os_)),
        grid=(n // tm,),
        in_specs=in_specs,
        out_specs=pl.BlockSpec((tm, d), lambda i: (i, 0)),
        out_shape=jax.ShapeDtypeStruct((n, d), F32),
        compiler_params=_params("parallel"),
        name="out_proj_ln",
    )(*os_, *ws, x2d, g, b)


def _mlp_ln_kernel(x_ref, wu_ref, wd_ref, g_ref, b_ref, y_ref, *, ff_chunk):
    x = x_ref[...]
    xb = x.astype(BF16)
    acc = None
    for c0 in range(0, wu_ref.shape[1], ff_chunk):
        h = jnp.maximum(_dot(xb, wu_ref[:, c0:c0 + ff_chunk]), 0.0)
        part = _dot((h * h).astype(BF16), wd_ref[c0:c0 + ff_chunk, :])
        acc = part if acc is None else acc + part
    y_ref[...] = _layer_norm(ALPHA * x + acc, g_ref[...], b_ref[...])


def _mlp_ln(x2d, wu, wd, g, b, tm=512, ff_chunk=1024):
    n, d = x2d.shape
    resident = pl.Buffered(1)
    return pl.pallas_call(
        functools.partial(_mlp_ln_kernel, ff_chunk=ff_chunk),
        grid=(n // tm,),
        in_specs=[pl.BlockSpec((tm, d), lambda i: (i, 0)),
                  pl.BlockSpec(wu.shape, lambda i: (0, 0), pipeline_mode=resident),
                  pl.BlockSpec(wd.shape, lambda i: (0, 0), pipeline_mode=resident),
                  pl.BlockSpec((1, d), lambda i: (0, 0)), pl.BlockSpec((1, d), lambda i: (0, 0))],
        out_specs=pl.BlockSpec((tm, d), lambda i: (i, 0)),
        out_shape=jax.ShapeDtypeStruct((n, d), F32),
        compiler_params=_params("parallel"),
        name="mlp_ln",
    )(x2d, wu, wd, g, b)


def _fox_cumsum_kernel(f_ref, bias_ref, tri_ref, scat_ref, onesrow_ref, cp_ref, kaug_ref, *, blk):
    tri = tri_ref[...]
    carry = jnp.zeros((1, LANES), F32)
    for r0 in range(0, f_ref.shape[1], blk):
        z = f_ref[0, r0:r0 + blk, :] + bias_ref[...]
        lf = jnp.minimum(z, 0.0) - jnp.log1p(jnp.exp(-jnp.abs(z)))
        p0, p1, p2 = _split3(lf)
        cs = _dot(tri, p0) + _dot(tri, p1) + _dot(tri, p2) + carry
        pieces = _split3(cs * LOG2E)
        kaug = onesrow_ref[...]
        for i, piece in enumerate(pieces):
            cp_ref[i, 0, r0:r0 + blk, :] = piece
            kaug = kaug + _dot(piece, scat_ref[i])
        kaug_ref[0, r0:r0 + blk, :] = kaug.astype(BF16)
        carry = cs[blk - 1:blk, :]


def _fox_cumsum(f3d, bias, blk=256):
    b, s, _ = f3d.shape
    blk = min(blk, s)
    tri = jnp.asarray(np.tril(np.ones((blk, blk), np.float32)), BF16)
    scat = np.zeros((3, LANES, LANES), np.float32)
    onesrow = np.zeros((1, LANES), np.float32)
    for half in (0, HEAD_DIM):
        for head in range(FOX_HEADS):
            for i in range(3):
                scat[i, head, half + 3 * head + i] = 1.0
        onesrow[0, half + FOX_ONE_LANE:half + FOX_ONE_LANE + 3] = 1.0
    return pl.pallas_call(
        functools.partial(_fox_cumsum_kernel, blk=blk),
        grid=(b,),
        in_specs=[pl.BlockSpec((1, s, LANES), lambda i: (i, 0, 0)),
                  pl.BlockSpec((1, LANES), lambda i: (0, 0)),
                  pl.BlockSpec((blk, blk), lambda i: (0, 0)),
                  pl.BlockSpec((3, LANES, LANES), lambda i: (0, 0, 0)),
                  pl.BlockSpec((1, LANES), lambda i: (0, 0))],
        out_specs=[pl.BlockSpec((3, 1, s, LANES), lambda i: (0, i, 0, 0)),
                   pl.BlockSpec((1, s, LANES), lambda i: (i, 0, 0))],
        out_shape=[jax.ShapeDtypeStruct((3, b, s, LANES), BF16),
                   jax.ShapeDtypeStruct((b, s, LANES), BF16)],
        compiler_params=_params("parallel"),
        name="fox_cumsum",
    )(f3d, bias, tri, jnp.asarray(scat, BF16), jnp.asarray(onesrow))


def _gate_lanes(gate_logits, r):
    sig = jax.nn.sigmoid(gate_logits)
    hi = sig.astype(BF16)
    lo = (sig - hi.astype(F32)).astype(BF16)
    return _dot(hi, r) + _dot(lo, r)


def _flash_t_kernel(*refs, mode, tk, qc, window, lam_init):
    it = iter(refs)
    qt_ref, k_ref, vt_ref = next(it), next(it), next(it)
    if mode in ("fox", "sel"):
        kaug_ref, qaugt_ref = next(it), next(it)
    if mode == "diff":
        lam_ref, subln_ref = next(it), next(it)
    if mode in ("sel", "win"):
        gate_ref, r_ref, prev_ref = next(it), next(it), next(it)
    eye_ref = next(it)
    o_ref, qa_ref, m_ref, acc_ref = next(it), next(it), next(it), next(it)

    seq = k_ref.shape[1]
    n_t, n_q = seq // tk, seq // qc
    vrows = acc_ref.shape[2] - ONES_ROWS

    qt = qt_ref[0]
    if mode in ("fox", "sel"):
        augs = (qaugt_ref[0, 0], qaugt_ref[0, 1])
    else:
        augs = (jnp.zeros((HEAD_DIM, seq), BF16),) * 2
    for qi in range(n_q):
        cs = slice(qi * qc, (qi + 1) * qc)
        qa_ref[0, qi] = jnp.concatenate([qt[0:HEAD_DIM, cs], augs[0][:, cs]], axis=0)
        qa_ref[1, qi] = jnp.concatenate([augs[1][:, cs], qt[HEAD_DIM:LANES, cs]], axis=0)
    m_ref[...] = jnp.full(m_ref.shape, NEG, F32)
    acc_ref[...] = jnp.zeros(acc_ref.shape, F32)

    lo_half_k = lax.broadcasted_iota(jnp.int32, (tk, LANES), 1) < HEAD_DIM
    ones = jnp.ones((ONES_ROWS, tk), BF16)
    n_back = window // tk

    def colmax(x):
        r = x.shape[0]
        while r > 8:
            r //= 2
            x = jnp.maximum(x[0:r], x[r:2 * r])
        return jnp.max(x, axis=0, keepdims=True)

    chains = []
    for d in range(n_t):
        for h in (0, 1):
            for qi in range(n_q):
                if (d <= qi <= d + n_back) if mode == "win" else (qi >= d):
                    chains.append((d, h, qi))

    key_tiles, value_tiles = {}, {}

    def key_tile(d, h):
        if (d, h) not in key_tiles:
            k0 = d * tk
            ka = k_ref[0, k0:k0 + tk, :]
            if mode in ("fox", "sel"):
                kaug = kaug_ref[0, k0:k0 + tk, :] if mode == "fox" else kaug_ref[k0:k0 + tk, :]
                ka = jnp.where(lo_half_k, ka, kaug) if h == 0 else jnp.where(lo_half_k, kaug, ka)
            key_tiles[(d, h)] = ka
        return key_tiles[(d, h)]

    def value_tile(d, h):
        if (d, h) not in value_tiles:
            k0 = d * tk
            vt = vt_ref[0, :, k0:k0 + tk] if mode == "diff" else vt_ref[0, h * HEAD_DIM:(h + 1) * HEAD_DIM, k0:k0 + tk]
            value_tiles[(d, h)] = jnp.concatenate([vt, ones], axis=0)
        return value_tiles[(d, h)]

    def scores(d, h, qi):
        k0 = d * tk
        s = _dot(key_tile(d, h), qa_ref[h, qi])
        if qi == d or (mode == "win" and qi == d + n_back):
            rows = lax.broadcasted_iota(jnp.int32, (tk, qc), 0)
            cols = lax.broadcasted_iota(jnp.int32, (tk, qc), 1)
            dist = (qi * qc - k0) + cols - rows
            if qi == d:
                s = jnp.where(dist >= 0, s, NEG)
            else:
                s = jnp.where(dist < window, s, NEG)
        return s

    pending = [scores(*ch) for ch in chains[:SCORE_LOOKAHEAD]]
    for idx, (d, h, qi) in enumerate(chains):
        s = pending.pop(0)
        m_prev = m_ref[h, qi]
        m_new = jnp.maximum(m_prev, colmax(s))
        p = jnp.exp2((s - m_new).astype(BF16))
        alpha = jnp.exp2(m_prev - m_new)
        acc_ref[h, qi] = alpha * acc_ref[h, qi] + _dot(value_tile(d, h), p)
        m_ref[h, qi] = m_new
        if idx + SCORE_LOOKAHEAD < len(chains):
            pending.append(scores(*chains[idx + SCORE_LOOKAHEAD]))

    if mode == "diff":
        lp = lam_ref[...]
        lam = (jnp.exp(jnp.sum(lp[0:1] * lp[1:2], axis=-1, keepdims=True))
               - jnp.exp(jnp.sum(lp[2:3] * lp[3:4], axis=-1, keepdims=True)) + lam_init)
    eye = eye_ref[...]
    for qi in range(n_q):
        c0 = qi * qc
        outs = [acc_ref[h, qi, 0:vrows, :] / acc_ref[h, qi, vrows:vrows + 1, :] for h in (0, 1)]
        o_t = outs[0] - lam * outs[1] if mode == "diff" else jnp.concatenate(outs, axis=0)
        hi = o_t.astype(BF16)
        lo = (o_t - hi.astype(F32)).astype(BF16)
        o = _dot_nt(eye, hi) + _dot_nt(eye, lo)
        if mode == "diff":
            ms = jnp.mean(o * o, axis=-1, keepdims=True)
            o = o * lax.rsqrt(ms + LN_EPS) * subln_ref[...] * (1.0 - lam_init)
        if mode in ("sel", "win"):
            o = prev_ref[0, c0:c0 + qc, :] + _gate_lanes(gate_ref[0, c0:c0 + qc, :], r_ref[0]) * o
        o_ref[0, c0:c0 + qc, :] = o.astype(o_ref.dtype)


def _flash_t(mode, qt_arr, q_off, k_arr, k_off, vt_arr, v_off, n_chunks, kv_per_chunk, out_dtype,
             extras=(), window=0, lam_init=0.0, tk=256, qc=256):
    b, s, _ = k_arr.shape
    tk, qc = min(tk, s), min(qc, s)
    k_map = (lambda bi, ci: (bi, 0, k_off + ci)) if kv_per_chunk else (lambda bi, ci: (bi, 0, k_off))
    v_map = (lambda bi, ci: (bi, v_off + ci, 0)) if kv_per_chunk else (lambda bi, ci: (bi, v_off, 0))
    eye = jnp.asarray(np.eye(qc, dtype=np.float32), BF16)
    in_specs = [pl.BlockSpec((1, LANES, s), lambda bi, ci: (bi, q_off + ci, 0)),
                pl.BlockSpec((1, s, LANES), k_map),
                pl.BlockSpec((1, LANES, s), v_map)]
    args = [qt_arr, k_arr, vt_arr]
    for arr, spec in extras:
        args.append(arr)
        in_specs.append(spec)
    args.append(eye)
    in_specs.append(pl.BlockSpec((qc, qc), lambda bi, ci: (0, 0)))
    acc_rows = (LANES if mode == "diff" else HEAD_DIM) + ONES_ROWS
    return pl.pallas_call(
        functools.partial(_flash_t_kernel, mode=mode, tk=tk, qc=qc, window=window, lam_init=lam_init),
        grid=(b, n_chunks),
        in_specs=in_specs,
        out_specs=pl.BlockSpec((1, s, LANES), lambda bi, ci: (bi, 0, ci)),
        out_shape=jax.ShapeDtypeStruct((b, s, n_chunks * LANES), out_dtype),
        scratch_shapes=[pltpu.VMEM((2, s // qc, LANES, qc), BF16), pltpu.VMEM((2, s // qc, 1, qc), F32),
                        pltpu.VMEM((2, s // qc, acc_rows, qc), F32)],
        compiler_params=_params("parallel", "parallel"),
        name="flash_" + mode,
    )(*args)


def _nsa_compress_kernel(tk_ref, tv_ref, w1a_ref, w1b_ref, pe_ref, b1_ref, w2_ref,
                         cos_ref, sa_ref, sb_ref, kc_ref, vc_ref):
    ncp = tk_ref.shape[1]
    for j, (t_ref, o_ref) in enumerate(((tk_ref, kc_ref), (tv_ref, vc_ref))):
        t = t_ref[0]
        u = _dot(t, w1a_ref[j])
        v = _dot(t, w1b_ref[j])
        const = _dot(pe_ref[j, 0], w1a_ref[j]) + _dot(pe_ref[j, 1], w1b_ref[j])
        pre = u + pltpu.roll(v, ncp - 1, 0) + const[0:1] + b1_ref[j]
        h = jax.nn.gelu(pre, approximate=True)
        out = _dot(h.astype(BF16), w2_ref[j])
        if j == 0:
            out = _rope(out, cos_ref[...], sa_ref[...], sb_ref[...])
        o_ref[0] = out.astype(o_ref.dtype)


def _nsa_compress(tkc, tvc, w1a, w1b, pe, b1, w2, tables_c):
    b, ncp, width = tkc.shape
    full = lambda a: pl.BlockSpec(a.shape, lambda i: (0,) * a.ndim)
    return pl.pallas_call(
        _nsa_compress_kernel,
        grid=(b,),
        in_specs=[pl.BlockSpec((1, ncp, width), lambda i: (i, 0, 0)),
                  pl.BlockSpec((1, ncp, width), lambda i: (i, 0, 0)),
                  full(w1a), full(w1b), full(pe), full(b1), full(w2)] + [full(t) for t in tables_c],
        out_specs=[pl.BlockSpec((1, ncp, LANES), lambda i: (i, 0, 0))] * 2,
        out_shape=[jax.ShapeDtypeStruct((b, ncp, LANES), BF16)] * 2,
        compiler_params=_params("parallel"),
        name="nsa_compress",
    )(tkc, tvc, w1a, w1b, pe, b1, w2, *tables_c)


def _nsa_cmp_kernel(q_ref, kc_ref, vc_ref, gate_ref, r_ref, ovt_ref, oc_ref, nsel_ref, *, tq, nsel, topk):
    qi = pl.program_id(1)
    kc = kc_ref[0]
    vc = vc_ref[0]
    ncp = kc.shape[0]
    ovt = ovt_ref[...]
    nsr = -(-nsel // 8) * 8
    lane = lax.broadcasted_iota(jnp.int32, (tq, LANES), 1)
    lo_half = lane < HEAD_DIM
    tpos_c = qi * tq + lax.broadcasted_iota(jnp.int32, (tq, ncp), 0)
    ncol = lax.broadcasted_iota(jnp.int32, (tq, ncp), 1)
    valid = (NSA_CMP_STRIDE * ncol + NSA_CMP_LEN - 1) <= tpos_c
    blk = lax.broadcasted_iota(jnp.int32, (nsr, tq), 0)
    cur = (qi * tq + lax.broadcasted_iota(jnp.int32, (nsr, tq), 1)) // NSA_SEL_LEN
    gates = jax.nn.sigmoid(gate_ref[0])
    g_hi = gates.astype(BF16)
    g_lo = (gates - g_hi.astype(F32)).astype(BF16)

    o_chunks = [None] * NSA_HPG
    gap = jnp.zeros((HEAD_DIM - nsr, tq), F32)
    for g in range(NSA_KV_HEADS):
        in_half = lo_half if g == 0 else jnp.logical_not(lo_half)
        psum = jnp.zeros((tq, ncp), F32)
        for j in range(NSA_HPG):
            qc = q_ref[0, :, j * LANES:(j + 1) * LANES]
            s = _dot_nt(jnp.where(in_half, qc, jnp.zeros_like(qc)), kc)
            s = jnp.where(valid, s, NEG)
            e = jnp.exp2(s - jnp.max(s, axis=-1, keepdims=True))
            p = jnp.where(valid, e / jnp.sum(e, axis=-1, keepdims=True), 0.0)
            psum = psum + p
            o = _dot(p.astype(BF16), vc)
            o_chunks[j] = o if g == 0 else jnp.where(lo_half, o_chunks[j], o)
        p_hi = psum.astype(BF16)
        p_lo = (psum - p_hi.astype(F32)).astype(BF16)
        imp = (_dot_nt(ovt, p_hi) + _dot_nt(ovt, p_lo))[0:nsr]
        imp = jnp.where(blk > cur, -NSA_BIG, imp)
        imp = jnp.where(blk == cur - 1, NSA_BIG, imp)
        imp = jnp.where(blk == cur, NSA_BIG, imp)
        imp = jnp.where(blk == 0, NSA_BIG, imp)
        imp = jnp.where(blk >= nsel, -3.0 * NSA_BIG, imp)
        rank = jnp.zeros((nsr, tq), jnp.int32)
        for jp in range(nsel):
            other = imp[jp:jp + 1, :]
            ahead = jnp.where(other > imp, 1, jnp.where(other == imp, jnp.where(blk > jp, 1, 0), 0))
            rank = rank + ahead
        nsel_ref[0, g] = jnp.concatenate([jnp.where(rank < topk, 0.0, 1.0), gap], axis=0).astype(nsel_ref.dtype)
    for j in range(NSA_HPG):
        r = r_ref[j * NSA_N_BRANCH]
        oc_ref[0, :, j * LANES:(j + 1) * LANES] = (_dot(g_hi, r) + _dot(g_lo, r)) * o_chunks[j]


def _nsa_cmp(q3d, q_blk, kc, vc, gate3d, r_tab, ovt, nsel, tq=256):
    b, s, _ = q3d.shape
    tq = min(tq, s)
    ncp = kc.shape[1]
    width = NSA_HPG * LANES
    full = lambda a: pl.BlockSpec(a.shape, lambda bi, qi: (0,) * a.ndim)
    return pl.pallas_call(
        functools.partial(_nsa_cmp_kernel, tq=tq, nsel=nsel, topk=min(NSA_TOPK, nsel)),
        grid=(b, s // tq),
        in_specs=[pl.BlockSpec((1, tq, width), lambda bi, qi: (bi, qi, q_blk)),
                  pl.BlockSpec((1, ncp, LANES), lambda bi, qi: (bi, 0, 0)),
                  pl.BlockSpec((1, ncp, LANES), lambda bi, qi: (bi, 0, 0)),
                  pl.BlockSpec((1, tq, LANES), lambda bi, qi: (bi, qi, 0)),
                  full(r_tab), full(ovt)],
        out_specs=[pl.BlockSpec((1, tq, width), lambda bi, qi: (bi, qi, 0)),
                   pl.BlockSpec((1, NSA_KV_HEADS, HEAD_DIM, tq), lambda bi, qi: (bi, 0, 0, qi))],
        out_shape=[jax.ShapeDtypeStruct((b, s, width), F32),
                   jax.ShapeDtypeStruct((b, NSA_KV_HEADS, HEAD_DIM, s), BF16)],
        compiler_params=_params("parallel", "arbitrary"),
        name="nsa_cmp_topk",
    )(q3d, kc, vc, gate3d, r_tab, ovt)


def _rope_tables(pos):
    half = ROPE_DIM // 2
    inv = ROPE_THETA ** (-jnp.arange(half, dtype=F32) / half)
    ang = pos.astype(F32)[:, None] * inv[None, :]
    cos, sin = jnp.cos(ang), jnp.sin(ang)
    ones = jnp.ones((pos.shape[0], HEAD_DIM - ROPE_DIM), F32)
    zeros = jnp.zeros((pos.shape[0], HEAD_DIM - ROPE_DIM), F32)
    zh = jnp.zeros_like(sin)
    c64 = jnp.concatenate([cos, cos, ones], axis=1)
    sa64 = jnp.concatenate([-sin, zh, zeros], axis=1)
    sb64 = jnp.concatenate([zh, sin, zeros], axis=1)
    return tuple(jnp.concatenate([t, t], axis=1) for t in (c64, sa64, sb64))


def _even_in_weight(w):
    o = np.cumsum([0, 512, 512, 512, 512, 128, 128, 128, 128, 128, 128, 24])
    qa, ka, va, qn, kc, vc, ksl, vsl, kw, vw, g = (np.arange(o[i], o[i + 1]) for i in range(11))
    qn_perm = np.concatenate([qn[(gi * NSA_HPG + j) * HEAD_DIM:(gi * NSA_HPG + j + 1) * HEAD_DIM]
                              for j in range(NSA_HPG) for gi in range(NSA_KV_HEADS)])
    cols = np.concatenate([qa, qn_perm, ka, ksl, kw, va, vsl, vw, kc, vc, g])
    scale = np.ones(cols.shape[0], np.float32)
    scale[0:1024] = HEAD_DIM ** -0.5 * LOG2E
    wp = jnp.take(w, jnp.asarray(cols), axis=1) * jnp.asarray(scale)[None, :]
    wp = jnp.pad(wp, ((0, 0), (0, LANES - g.shape[0])))
    return wp.astype(BF16)


def _even_out_weight(w):
    rows_b = np.concatenate([512 + np.arange((gi * NSA_HPG + j) * HEAD_DIM, (gi * NSA_HPG + j + 1) * HEAD_DIM)
                             for j in range(NSA_HPG) for gi in range(NSA_KV_HEADS)])
    return w[:512].astype(BF16), jnp.take(w, jnp.asarray(rows_b), axis=0).astype(BF16)


def _odd_in_weight(w):
    scale = np.ones(w.shape[1], np.float32)
    scale[:D_MODEL] = HEAD_DIM ** -0.5 * LOG2E
    wp = w * jnp.asarray(scale)[None, :]
    return jnp.pad(wp, ((0, 0), (0, LANES - FOX_HEADS))).astype(BF16)


def _compress_weights(pe, w1, b1, w2):
    eye = jnp.eye(NSA_KV_HEADS, dtype=F32)
    w1r = w1.reshape(2, 2, NSA_CMP_STRIDE, HEAD_DIM, NSA_CMP_HIDDEN)
    w1x = jnp.einsum("jardc,gh->jargdhc", w1r, eye)
    w1x = w1x.reshape(2, 2, NSA_CMP_STRIDE * LANES, NSA_KV_HEADS * NSA_CMP_HIDDEN)
    pe_r = pe.reshape(2, 2, NSA_CMP_STRIDE, 1, HEAD_DIM)
    pe_x = jnp.broadcast_to(pe_r, (2, 2, NSA_CMP_STRIDE, NSA_KV_HEADS, HEAD_DIM)).reshape(2, 2, 1, NSA_CMP_STRIDE * LANES)
    pe_x = jnp.broadcast_to(pe_x, (2, 2, 8, NSA_CMP_STRIDE * LANES))
    b1x = jnp.tile(b1, (1, NSA_KV_HEADS)).reshape(2, 1, NSA_KV_HEADS * NSA_CMP_HIDDEN)
    w2x = jnp.einsum("jcd,gh->jgchd", w2, eye).reshape(2, NSA_KV_HEADS * NSA_CMP_HIDDEN, LANES)
    return (w1x[:, 0].astype(BF16), w1x[:, 1].astype(BF16), pe_x.astype(BF16), b1x.astype(F32), w2x.astype(BF16))


def _nsa_constants(seq):
    ncp = seq // NSA_CMP_STRIDE
    nsel = seq // NSA_SEL_LEN
    n = np.arange(ncp)[:, None]
    j = np.arange(LANES)[None, :]
    cs, ce = n * NSA_CMP_STRIDE, n * NSA_CMP_STRIDE + NSA_CMP_LEN - 1
    ss = j * NSA_SEL_LEN
    ov = ((cs <= ss + NSA_SEL_LEN - 1) & (ce >= ss) & (j < nsel) & (n < ncp - 1)).astype(np.float32)
    tok = np.arange(seq)
    eneg = -MASK_BIG * (tok[:, None] // NSA_SEL_LEN == (np.arange(LANES) % HEAD_DIM)[None, :]).astype(np.float32)
    r = np.zeros((NSA_HPG * NSA_N_BRANCH, LANES, LANES), np.float32)
    for jc in range(NSA_HPG):
        for br in range(NSA_N_BRANCH):
            for ln in range(LANES):
                head = (ln // HEAD_DIM) * NSA_HPG + jc
                r[jc * NSA_N_BRANCH + br, head * NSA_N_BRANCH + br, ln] = 1.0
    return jnp.asarray(ov.T.copy(), BF16), jnp.asarray(eneg, BF16), jnp.asarray(r, BF16), nsel


def _fox_query_aug(c_pieces):
    _, b, s, _ = c_pieces.shape
    pieces = jnp.transpose(c_pieces[..., :FOX_HEADS], (1, 3, 0, 2))
    own = -(np.arange(FOX_ONE_LANE)[None, :] // 3 == np.arange(FOX_HEADS)[:, None]).astype(np.float32)
    minus = jnp.broadcast_to(jnp.asarray(own, BF16)[None, :, :, None], (b, FOX_HEADS, FOX_ONE_LANE, s))
    pad = jnp.zeros((b, FOX_HEADS, FOX_AUG_ROWS - FOX_ONE_LANE - 3, s), BF16)
    return jnp.concatenate([minus, pieces, pad], axis=2)


def kernel(x, ln_gain, ln_bias, mlp_w_up, mlp_w_down, w_in_even, w_out_even, diff_lambda, diff_subln,
           nsa_pe, nsa_cmp_w1, nsa_cmp_b1, nsa_cmp_w2, w_in_odd, fox_f_bias, w_out_odd):
    b, s, d = x.shape
    n = b * s
    ncp = s // NSA_CMP_STRIDE
    xf = x.reshape(n, d)
    tables = _rope_tables(jnp.arange(s))
    tables_c = _rope_tables(jnp.arange(ncp) * NSA_CMP_STRIDE + NSA_CMP_LEN - 1)
    ovt, eneg, r_tab, nsel = _nsa_constants(s)
    row = lambda a: a.reshape(1, -1).astype(F32)
    t3 = lambda a: jnp.transpose(a.reshape(b, s, -1), (0, 2, 1))
    full2 = lambda a: pl.BlockSpec(a.shape, lambda bi, ci: (0, 0))

    for layer in range(DEPTH):
        li = layer // 2
        if layer % 2 == 0:
            q, k, vv, kci, vci, gate = _proj(xf, _even_in_weight(w_in_even[li]), tables, EVEN_SEGS, EVEN_DTYPES, s)
            q3, qt3, k3, vt3, gate3 = q.reshape(b, s, -1), t3(q), k.reshape(b, s, -1), t3(vv), gate.reshape(b, s, LANES)
            lam_init = 0.8 - 0.6 * math.exp(-0.3 * layer)
            lam_p, subln = diff_lambda[li].astype(F32), row(diff_subln[li])
            o_a = _flash_t("diff", qt3, 0, k3, 0, vt3, 0, DIFF_HEADS, True, BF16,
                           extras=((lam_p, full2(lam_p)), (subln, full2(subln))), lam_init=lam_init)
            w1a, w1b, pe_x, b1x, w2x = _compress_weights(nsa_pe[li], nsa_cmp_w1[li], nsa_cmp_b1[li], nsa_cmp_w2[li])
            kc, vc = _nsa_compress(kci.reshape(b, ncp, -1), vci.reshape(b, ncp, -1), w1a, w1b, pe_x, b1x, w2x, tables_c)
            o_c, notsel_t = _nsa_cmp(q3, 1, kc, vc, gate3, r_tab, ovt, nsel)
            gate_spec = pl.BlockSpec((1, s, LANES), lambda bi, ci: (bi, 0, 0))
            prev_spec = pl.BlockSpec((1, s, LANES), lambda bi, ci: (bi, 0, ci))
            r_spec = lambda br: pl.BlockSpec((1, LANES, LANES), lambda bi, ci: (ci * NSA_N_BRANCH + br, 0, 0))
            nsel_spec = pl.BlockSpec((1, NSA_KV_HEADS, HEAD_DIM, s), lambda bi, ci: (bi, 0, 0, 0))
            o_cs = _flash_t("sel", qt3, 4, k3, 4, vt3, 4, NSA_HPG, False, F32,
                            extras=((eneg, full2(eneg)), (notsel_t, nsel_spec), (gate3, gate_spec), (r_tab, r_spec(1)),
                                    (o_c, prev_spec)))
            o_b = _flash_t("win", qt3, 4, k3, 5, vt3, 5, NSA_HPG, False, BF16,
                           extras=((gate3, gate_spec), (r_tab, r_spec(2)), (o_cs, prev_spec)), window=NSA_WINDOW)
            wo_a, wo_b = _even_out_weight(w_out_even[li])
            xf = _outproj_ln([o_a.reshape(n, -1), o_b.reshape(n, -1)], [wo_a, wo_b], xf,
                             row(ln_gain[layer, 0]), row(ln_bias[layer, 0]))
        else:
            q, k, v, fl = _proj(xf, _odd_in_weight(w_in_odd[li]), None, ODD_SEGS, ODD_DTYPES, s)
            bias = jnp.pad(fox_f_bias[li].astype(F32), (0, LANES - FOX_HEADS)).reshape(1, LANES)
            c_pieces, kaug = _fox_cumsum(fl.reshape(b, s, LANES), bias)
            qaug_t = _fox_query_aug(c_pieces)
            kaug_spec = pl.BlockSpec((1, s, LANES), lambda bi, ci: (bi, 0, 0))
            qaug_spec = pl.BlockSpec((1, 2, FOX_AUG_ROWS, s), lambda bi, ci: (bi, ci, 0, 0))
            o_f = _flash_t("fox", t3(q), 0, k.reshape(b, s, -1), 0, t3(v), 0, FOX_HEADS // 2, True, BF16,
                           extras=((kaug, kaug_spec), (qaug_t, qaug_spec)))
            xf = _outproj_ln([o_f.reshape(n, -1)], [w_out_odd[li].astype(BF16)], xf,
                             row(ln_gain[layer, 0]), row(ln_bias[layer, 0]))
        xf = _mlp_ln(xf, mlp_w_up[layer].astype(BF16), mlp_w_down[layer].astype(BF16),
                     row(ln_gain[layer, 1]), row(ln_bias[layer, 1]))
    return xf.reshape(b, s, d)
```

```python
import functools
import math

import numpy as np
import jax
import jax.numpy as jnp
from jax import lax
from jax.experimental import pallas as pl
from jax.experimental.pallas import tpu as pltpu

F32 = jnp.float32
BF16 = jnp.bfloat16

D_MODEL = 1024
DEPTH = 4
HEAD_DIM = 64
ROPE_DIM = HEAD_DIM // 4
ROPE_THETA = 500000.0
LN_EPS = 1e-5
DIFF_HEADS = 4
NSA_HEADS = 8
NSA_KV_HEADS = 2
NSA_HPG = NSA_HEADS // NSA_KV_HEADS
NSA_CMP_LEN = 32
NSA_CMP_STRIDE = 16
NSA_CMP_HIDDEN = 256
NSA_SEL_LEN = 64
NSA_TOPK = 8
NSA_WINDOW = 512
NSA_N_BRANCH = 3
NSA_BIG = 1e4
FOX_HEADS = D_MODEL // HEAD_DIM
D_FF = 4 * D_MODEL
ALPHA = (2 * DEPTH) ** 0.25

LANES = 128
NEG = -1e30
VMEM_LIMIT = 56 * 1024 * 1024
LOG2E = 1.4426950408889634
MASK_BIG = 30000.0
FOX_ONE_LANE = 3 * 16
FOX_AUG_ROWS = HEAD_DIM
ONES_ROWS = 16
SCORE_LOOKAHEAD = 8

EVEN_SEGS = ((0, 1024, True), (1024, 768, True), (1792, 768, False),
             (2560, 128, False), (2688, 128, False), (2816, 128, False))
EVEN_DTYPES = (BF16, BF16, BF16, BF16, BF16, F32)
ODD_SEGS = ((0, D_MODEL, False), (D_MODEL, D_MODEL, False), (2 * D_MODEL, D_MODEL, False),
            (3 * D_MODEL, 128, False))
ODD_DTYPES = (BF16, BF16, BF16, F32)


def _dot(a, b):
    return jnp.dot(a, b, preferred_element_type=F32)


def _dot_nt(a, b):
    return lax.dot_general(a, b, (((1,), (1,)), ((), ())), preferred_element_type=F32)


def _split3(x):
    p0 = x.astype(BF16)
    r1 = x - p0.astype(F32)
    p1 = r1.astype(BF16)
    p2 = (r1 - p1.astype(F32)).astype(BF16)
    return p0, p1, p2


def _rope(h, cos, sa, sb):
    return h * cos + pltpu.roll(h, LANES - ROPE_DIM // 2, 1) * sa + pltpu.roll(h, ROPE_DIM // 2, 1) * sb


def _layer_norm(y, g, b):
    mu = jnp.mean(y, axis=-1, keepdims=True)
    yc = y - mu
    var = jnp.mean(yc * yc, axis=-1, keepdims=True)
    return yc * lax.rsqrt(var + LN_EPS) * g + b


def _params(*sem):
    return pltpu.CompilerParams(dimension_semantics=sem, vmem_limit_bytes=VMEM_LIMIT)


def _proj_kernel(*refs, segs, has_rope):
    x_ref, w_ref = refs[0], refs[1]
    pos = 2
    if has_rope:
        cos_ref, sa_ref, sb_ref = refs[2:5]
        pos = 5
    out_refs = refs[pos:]
    xb = x_ref[...].astype(BF16)
    for o_ref, (c0, width, rope) in zip(out_refs, segs):
        for cc in range(0, width, 512):
            w = min(512, width - cc)
            h = _dot(xb, w_ref[:, c0 + cc:c0 + cc + w])
            if rope:
                cos, sa, sb = cos_ref[...], sa_ref[...], sb_ref[...]
                for t in range(0, w, LANES):
                    o_ref[:, cc + t:cc + t + LANES] = _rope(h[:, t:t + LANES], cos, sa, sb).astype(o_ref.dtype)
            else:
                o_ref[:, cc:cc + w] = h.astype(o_ref.dtype)


def _proj(x2d, w, tables, segs, dtypes, seq, tm=512):
    n, d = x2d.shape
    m = w.shape[1]
    ns = seq // tm
    has_rope = tables is not None
    in_specs = [pl.BlockSpec((tm, d), lambda i: (i, 0)), pl.BlockSpec((d, m), lambda i: (0, 0))]
    args = [x2d, w]
    if has_rope:
        in_specs += [pl.BlockSpec((tm, LANES), lambda i: (i % ns, 0))] * 3
        args += list(tables)
    return pl.pallas_call(
        functools.partial(_proj_kernel, segs=segs, has_rope=has_rope),
        grid=(n // tm,),
        in_specs=in_specs,
        out_specs=[pl.BlockSpec((tm, s[1]), lambda i: (i, 0)) for s in segs],
        out_shape=[jax.ShapeDtypeStruct((n, s[1]), dt) for s, dt in zip(segs, dtypes)],
        compiler_params=_params("parallel"),
        name="in_proj",
    )(*args)


def _mix_mlp_kernel(*refs, n_in, ff_chunk):
    o_refs = refs[:n_in]
    w_refs = refs[n_in:2 * n_in]
    x_ref, wu_ref, wd_ref, g1_ref, b1_ref, g2_ref, b2_ref, y_ref = refs[2 * n_in:]
    mix = _dot(o_refs[0][...], w_refs[0][...])
    for o_ref, w_ref in zip(o_refs[1:], w_refs[1:]):
        mix = mix + _dot(o_ref[...], w_ref[...])
    x = _layer_norm(ALPHA * x_ref[...] + mix, g1_ref[...], b1_ref[...])
    xb = x.astype(BF16)
    acc = None
    for c0 in range(0, wu_ref.shape[1], ff_chunk):
        h = jnp.maximum(_dot(xb, wu_ref[:, c0:c0 + ff_chunk]), 0.0)
        part = _dot((h * h).astype(BF16), wd_ref[c0:c0 + ff_chunk, :])
        acc = part if acc is None else acc + part
    y_ref[...] = _layer_norm(ALPHA * x + acc, g2_ref[...], b2_ref[...])


def _mix_mlp(os_, ws, x2d, wu, wd, g1, b1, g2, b2, tm=512, ff_chunk=1024):
    n, d = x2d.shape
    resident = lambda a: pl.BlockSpec(a.shape, lambda i: (0, 0), pipeline_mode=pl.Buffered(1))
    rowvec = pl.BlockSpec((1, d), lambda i: (0, 0))
    in_specs = [pl.BlockSpec((tm, o.shape[1]), lambda i: (i, 0)) for o in os_]
    in_specs += [resident(w) for w in ws]
    in_specs += [pl.BlockSpec((tm, d), lambda i: (i, 0)), resident(wu), resident(wd), rowvec, rowvec, rowvec, rowvec]
    return pl.pallas_call(
        functools.partial(_mix_mlp_kernel, n_in=len(os_), ff_chunk=ff_chunk),
        grid=(n // tm,),
        in_specs=in_specs,
        out_specs=pl.BlockSpec((tm, d), lambda i: (i, 0)),
        out_shape=jax.ShapeDtypeStruct((n, d), F32),
        compiler_params=_params("parallel"),
        name="mix_mlp_ln",
    )(*os_, *ws, x2d, wu, wd, g1, b1, g2, b2)


def _fox_cumsum_kernel(f_ref, bias_ref, tri_ref, scat_ref, onesrow_ref, cp_ref, kaug_ref, *, blk):
    tri = tri_ref[...]
    carry = jnp.zeros((1, LANES), F32)
    for r0 in range(0, f_ref.shape[1], blk):
        z = f_ref[0, r0:r0 + blk, :] + bias_ref[...]
        lf = jnp.minimum(z, 0.0) - jnp.log1p(jnp.exp(-jnp.abs(z)))
        p0, p1, p2 = _split3(lf)
        cs = _dot(tri, p0) + _dot(tri, p1) + _dot(tri, p2) + carry
        pieces = _split3(cs * LOG2E)
        kaug = onesrow_ref[...]
        for i, piece in enumerate(pieces):
            cp_ref[i, 0, r0:r0 + blk, :] = piece
            kaug = kaug + _dot(piece, scat_ref[i])
        kaug_ref[0, r0:r0 + blk, :] = kaug.astype(BF16)
        carry = cs[blk - 1:blk, :]


def _fox_cumsum(f3d, bias, blk=256):
    b, s, _ = f3d.shape
    blk = min(blk, s)
    tri = jnp.asarray(np.tril(np.ones((blk, blk), np.float32)), BF16)
    scat = np.zeros((3, LANES, LANES), np.float32)
    onesrow = np.zeros((1, LANES), np.float32)
    for half in (0, HEAD_DIM):
        for head in range(FOX_HEADS):
            for i in range(3):
                scat[i, head, half + 3 * head + i] = 1.0
        onesrow[0, half + FOX_ONE_LANE:half + FOX_ONE_LANE + 3] = 1.0
    return pl.pallas_call(
        functools.partial(_fox_cumsum_kernel, blk=blk),
        grid=(b,),
        in_specs=[pl.BlockSpec((1, s, LANES), lambda i: (i, 0, 0)),
                  pl.BlockSpec((1, LANES), lambda i: (0, 0)),
                  pl.BlockSpec((blk, blk), lambda i: (0, 0)),
                  pl.BlockSpec((3, LANES, LANES), lambda i: (0, 0, 0)),
                  pl.BlockSpec((1, LANES), lambda i: (0, 0))],
        out_specs=[pl.BlockSpec((3, 1, s, LANES), lambda i: (0, i, 0, 0)),
                   pl.BlockSpec((1, s, LANES), lambda i: (i, 0, 0))],
        out_shape=[jax.ShapeDtypeStruct((3, b, s, LANES), BF16),
                   jax.ShapeDtypeStruct((b, s, LANES), BF16)],
        compiler_params=_params("parallel"),
        name="fox_cumsum",
    )(f3d, bias, tri, jnp.asarray(scat, BF16), jnp.asarray(onesrow))


def _gate_lanes(gate_logits, r):
    sig = jax.nn.sigmoid(gate_logits)
    hi = sig.astype(BF16)
    lo = (sig - hi.astype(F32)).astype(BF16)
    return _dot(hi, r) + _dot(lo, r)


def _flash_t_kernel(*refs, mode, tk, qc, window, lam_init):
    it = iter(refs)
    qt_ref, k_ref, vt_ref = next(it), next(it), next(it)
    if mode in ("fox", "sel"):
        kaug_ref, qaugt_ref = next(it), next(it)
    if mode == "diff":
        lam_ref, subln_ref = next(it), next(it)
    if mode in ("sel", "win"):
        gate_ref, r_ref, prev_ref = next(it), next(it), next(it)
    eye_ref = next(it)
    o_ref, qa_ref, m_ref, acc_ref = next(it), next(it), next(it), next(it)

    seq = k_ref.shape[1]
    n_t, n_q = seq // tk, seq // qc
    vrows = acc_ref.shape[2] - ONES_ROWS

    qt = qt_ref[0]
    if mode == "fox":
        row = lax.broadcasted_iota(jnp.int32, (FOX_ONE_LANE, seq), 0)
        augs = []
        for h in (0, 1):
            own = (row - 3 * (2 * pl.program_id(1) + h)).astype(jnp.uint32) < 3
            augs.append(jnp.concatenate([jnp.where(own, -1.0, 0.0).astype(BF16), qaugt_ref[0, h]], axis=0))
    elif mode == "sel":
        augs = (qaugt_ref[0, 0], qaugt_ref[0, 1])
    else:
        augs = (jnp.zeros((HEAD_DIM, seq), BF16),) * 2
    for qi in range(n_q):
        cs = slice(qi * qc, (qi + 1) * qc)
        qa_ref[0, qi] = jnp.concatenate([qt[0:HEAD_DIM, cs], augs[0][:, cs]], axis=0)
        qa_ref[1, qi] = jnp.concatenate([augs[1][:, cs], qt[HEAD_DIM:LANES, cs]], axis=0)
    m_ref[...] = jnp.full(m_ref.shape, NEG, F32)
    acc_ref[...] = jnp.zeros(acc_ref.shape, F32)

    lo_half_k = lax.broadcasted_iota(jnp.int32, (tk, LANES), 1) < HEAD_DIM
    ones = jnp.ones((ONES_ROWS, tk), BF16)
    n_back = window // tk

    def colmax(x):
        r = x.shape[0]
        while r > 8:
            r //= 2
            x = jnp.maximum(x[0:r], x[r:2 * r])
        return jnp.max(x, axis=0, keepdims=True)

    chains = []
    for d in range(n_t):
        for h in (0, 1):
            for qi in range(n_q):
                if (d <= qi <= d + n_back) if mode == "win" else (qi >= d):
                    chains.append((d, h, qi))

    key_tiles, value_tiles = {}, {}

    def key_tile(d, h):
        if (d, h) not in key_tiles:
            k0 = d * tk
            ka = k_ref[0, k0:k0 + tk, :]
            if mode in ("fox", "sel"):
                kaug = kaug_ref[0, k0:k0 + tk, :] if mode == "fox" else kaug_ref[k0:k0 + tk, :]
                ka = jnp.where(lo_half_k, ka, kaug) if h == 0 else jnp.where(lo_half_k, kaug, ka)
            key_tiles[(d, h)] = ka
        return key_tiles[(d, h)]

    def value_tile(d, h):
        if (d, h) not in value_tiles:
            k0 = d * tk
            vt = vt_ref[0, :, k0:k0 + tk] if mode == "diff" else vt_ref[0, h * HEAD_DIM:(h + 1) * HEAD_DIM, k0:k0 + tk]
            value_tiles[(d, h)] = jnp.concatenate([vt, ones], axis=0)
        return value_tiles[(d, h)]

    def scores(d, h, qi):
        k0 = d * tk
        s = _dot(key_tile(d, h), qa_ref[h, qi])
        if qi == d or (mode == "win" and qi == d + n_back):
            rows = lax.broadcasted_iota(jnp.int32, (tk, qc), 0)
            cols = lax.broadcasted_iota(jnp.int32, (tk, qc), 1)
            dist = (qi * qc - k0) + cols - rows
            if qi == d:
                s = jnp.where(dist >= 0, s, NEG)
            else:
                s = jnp.where(dist < window, s, NEG)
        return s

    pending = [scores(*ch) for ch in chains[:SCORE_LOOKAHEAD]]
    for idx, (d, h, qi) in enumerate(chains):
        s = pending.pop(0)
        m_prev = m_ref[h, qi]
        m_new = jnp.maximum(m_prev, colmax(s))
        p = jnp.exp2((s - m_new).astype(BF16))
        alpha = jnp.exp2(m_prev - m_new)
        acc_ref[h, qi] = alpha * acc_ref[h, qi] + _dot(value_tile(d, h), p)
        m_ref[h, qi] = m_new
        if idx + SCORE_LOOKAHEAD < len(chains):
            pending.append(scores(*chains[idx + SCORE_LOOKAHEAD]))

    if mode == "diff":
        lp = lam_ref[...]
        lam = (jnp.exp(jnp.sum(lp[0:1] * lp[1:2], axis=-1, keepdims=True))
               - jnp.exp(jnp.sum(lp[2:3] * lp[3:4], axis=-1, keepdims=True)) + lam_init)
    eye = eye_ref[...]
    for qi in range(n_q):
        c0 = qi * qc
        outs = [acc_ref[h, qi, 0:vrows, :] / acc_ref[h, qi, vrows:vrows + 1, :] for h in (0, 1)]
        o_t = outs[0] - lam * outs[1] if mode == "diff" else jnp.concatenate(outs, axis=0)
        hi = o_t.astype(BF16)
        lo = (o_t - hi.astype(F32)).astype(BF16)
        o = _dot_nt(eye, hi) + _dot_nt(eye, lo)
        if mode == "diff":
            ms = jnp.mean(o * o, axis=-1, keepdims=True)
            o = o * lax.rsqrt(ms + LN_EPS) * subln_ref[...] * (1.0 - lam_init)
        if mode in ("sel", "win"):
            o = prev_ref[0, c0:c0 + qc, :] + _gate_lanes(gate_ref[0, c0:c0 + qc, :], r_ref[0]) * o
        o_ref[0, c0:c0 + qc, :] = o.astype(o_ref.dtype)


def _flash_t(mode, qt_arr, q_off, k_arr, k_off, vt_arr, v_off, n_chunks, kv_per_chunk, out_dtype,
             extras=(), window=0, lam_init=0.0, tk=256, qc=256):
    b, s, _ = k_arr.shape
    tk, qc = min(tk, s), min(qc, s)
    k_map = (lambda bi, ci: (bi, 0, k_off + ci)) if kv_per_chunk else (lambda bi, ci: (bi, 0, k_off))
    v_map = (lambda bi, ci: (bi, v_off + ci, 0)) if kv_per_chunk else (lambda bi, ci: (bi, v_off, 0))
    eye = jnp.asarray(np.eye(qc, dtype=np.float32), BF16)
    in_specs = [pl.BlockSpec((1, LANES, s), lambda bi, ci: (bi, q_off + ci, 0)),
                pl.BlockSpec((1, s, LANES), k_map),
                pl.BlockSpec((1, LANES, s), v_map)]
    args = [qt_arr, k_arr, vt_arr]
    for arr, spec in extras:
        args.append(arr)
        in_specs.append(spec)
    args.append(eye)
    in_specs.append(pl.BlockSpec((qc, qc), lambda bi, ci: (0, 0)))
    acc_rows = (LANES if mode == "diff" else HEAD_DIM) + ONES_ROWS
    return pl.pallas_call(
        functools.partial(_flash_t_kernel, mode=mode, tk=tk, qc=qc, window=window, lam_init=lam_init),
        grid=(b, n_chunks),
        in_specs=in_specs,
        out_specs=pl.BlockSpec((1, s, LANES), lambda bi, ci: (bi, 0, ci)),
        out_shape=jax.ShapeDtypeStruct((b, s, n_chunks * LANES), out_dtype),
        scratch_shapes=[pltpu.VMEM((2, s // qc, LANES, qc), BF16), pltpu.VMEM((2, s // qc, 1, qc), F32),
                        pltpu.VMEM((2, s // qc, acc_rows, qc), F32)],
        compiler_params=_params("parallel", "parallel"),
        name="flash_" + mode,
    )(*args)


def _nsa_compress_kernel(tk_ref, tv_ref, w1a_ref, w1b_ref, pe_ref, b1_ref, w2_ref,
                         cos_ref, sa_ref, sb_ref, kc_ref, vc_ref):
    ncp = tk_ref.shape[1]
    for j, (t_ref, o_ref) in enumerate(((tk_ref, kc_ref), (tv_ref, vc_ref))):
        t = t_ref[0]
        u = _dot(t, w1a_ref[j])
        v = _dot(t, w1b_ref[j])
        const = _dot(pe_ref[j, 0], w1a_ref[j]) + _dot(pe_ref[j, 1], w1b_ref[j])
        pre = u + pltpu.roll(v, ncp - 1, 0) + const[0:1] + b1_ref[j]
        h = jax.nn.gelu(pre, approximate=True)
        out = _dot(h.astype(BF16), w2_ref[j])
        if j == 0:
            out = _rope(out, cos_ref[...], sa_ref[...], sb_ref[...])
        o_ref[0] = out.astype(o_ref.dtype)


def _nsa_compress(tkc, tvc, w1a, w1b, pe, b1, w2, tables_c):
    b, ncp, width = tkc.shape
    full = lambda a: pl.BlockSpec(a.shape, lambda i: (0,) * a.ndim)
    return pl.pallas_call(
        _nsa_compress_kernel,
        grid=(b,),
        in_specs=[pl.BlockSpec((1, ncp, width), lambda i: (i, 0, 0)),
                  pl.BlockSpec((1, ncp, width), lambda i: (i, 0, 0)),
                  full(w1a), full(w1b), full(pe), full(b1), full(w2)] + [full(t) for t in tables_c],
        out_specs=[pl.BlockSpec((1, ncp, LANES), lambda i: (i, 0, 0))] * 2,
        out_shape=[jax.ShapeDtypeStruct((b, ncp, LANES), BF16)] * 2,
        compiler_params=_params("parallel"),
        name="nsa_compress",
    )(tkc, tvc, w1a, w1b, pe, b1, w2, *tables_c)


def _nsa_cmp_kernel(q_ref, kc_ref, vc_ref, gate_ref, r_ref, ovt_ref, oc_ref, nsel_ref, *, tq, nsel, topk):
    qi = pl.program_id(1)
    kc = kc_ref[0]
    vc = vc_ref[0]
    ncp = kc.shape[0]
    ovt = ovt_ref[...]
    nsr = -(-nsel // 8) * 8
    lane = lax.broadcasted_iota(jnp.int32, (tq, LANES), 1)
    lo_half = lane < HEAD_DIM
    tpos_c = qi * tq + lax.broadcasted_iota(jnp.int32, (tq, ncp), 0)
    ncol = lax.broadcasted_iota(jnp.int32, (tq, ncp), 1)
    valid = (NSA_CMP_STRIDE * ncol + NSA_CMP_LEN - 1) <= tpos_c
    blk = lax.broadcasted_iota(jnp.int32, (nsr, tq), 0)
    cur = (qi * tq + lax.broadcasted_iota(jnp.int32, (nsr, tq), 1)) // NSA_SEL_LEN
    gates = jax.nn.sigmoid(gate_ref[0])
    g_hi = gates.astype(BF16)
    g_lo = (gates - g_hi.astype(F32)).astype(BF16)

    o_chunks = [None] * NSA_HPG
    gap = jnp.zeros((HEAD_DIM - nsr, tq), F32)
    for g in range(NSA_KV_HEADS):
        in_half = lo_half if g == 0 else jnp.logical_not(lo_half)
        psum = jnp.zeros((tq, ncp), F32)
        for j in range(NSA_HPG):
            qc = q_ref[0, :, j * LANES:(j + 1) * LANES]
            s = _dot_nt(jnp.where(in_half, qc, jnp.zeros_like(qc)), kc)
            s = jnp.where(valid, s, NEG)
            e = jnp.exp2(s - jnp.max(s, axis=-1, keepdims=True))
            p = jnp.where(valid, e / jnp.sum(e, axis=-1, keepdims=True), 0.0)
            psum = psum + p
            o = _dot(p.astype(BF16), vc)
            o_chunks[j] = o if g == 0 else jnp.where(lo_half, o_chunks[j], o)
        p_hi = psum.astype(BF16)
        p_lo = (psum - p_hi.astype(F32)).astype(BF16)
        imp = (_dot_nt(ovt, p_hi) + _dot_nt(ovt, p_lo))[0:nsr]
        imp = jnp.where(blk > cur, -NSA_BIG, imp)
        imp = jnp.where(blk == cur - 1, NSA_BIG, imp)
        imp = jnp.where(blk == cur, NSA_BIG, imp)
        imp = jnp.where(blk == 0, NSA_BIG, imp)
        imp = jnp.where(blk >= nsel, -3.0 * NSA_BIG, imp)
        rank = jnp.zeros((nsr, tq), jnp.int32)
        for jp in range(nsel):
            other = imp[jp:jp + 1, :]
            ahead = jnp.where(other > imp, 1, jnp.where(other == imp, jnp.where(blk > jp, 1, 0), 0))
            rank = rank + ahead
        nsel_ref[0, g] = jnp.concatenate([jnp.where(rank < topk, 0.0, 1.0), gap], axis=0).astype(nsel_ref.dtype)
    for j in range(NSA_HPG):
        r = r_ref[j * NSA_N_BRANCH]
        oc_ref[0, :, j * LANES:(j + 1) * LANES] = (_dot(g_hi, r) + _dot(g_lo, r)) * o_chunks[j]


def _nsa_cmp(q3d, q_blk, kc, vc, gate3d, r_tab, ovt, nsel, tq=256):
    b, s, _ = q3d.shape
    tq = min(tq, s)
    ncp = kc.shape[1]
    width = NSA_HPG * LANES
    full = lambda a: pl.BlockSpec(a.shape, lambda bi, qi: (0,) * a.ndim)
    return pl.pallas_call(
        functools.partial(_nsa_cmp_kernel, tq=tq, nsel=nsel, topk=min(NSA_TOPK, nsel)),
        grid=(b, s // tq),
        in_specs=[pl.BlockSpec((1, tq, width), lambda bi, qi: (bi, qi, q_blk)),
                  pl.BlockSpec((1, ncp, LANES), lambda bi, qi: (bi, 0, 0)),
                  pl.BlockSpec((1, ncp, LANES), lambda bi, qi: (bi, 0, 0)),
                  pl.BlockSpec((1, tq, LANES), lambda bi, qi: (bi, qi, 0)),
                  full(r_tab), full(ovt)],
        out_specs=[pl.BlockSpec((1, tq, width), lambda bi, qi: (bi, qi, 0)),
                   pl.BlockSpec((1, NSA_KV_HEADS, HEAD_DIM, tq), lambda bi, qi: (bi, 0, 0, qi))],
        out_shape=[jax.ShapeDtypeStruct((b, s, width), F32),
                   jax.ShapeDtypeStruct((b, NSA_KV_HEADS, HEAD_DIM, s), BF16)],
        compiler_params=_params("parallel", "arbitrary"),
        name="nsa_cmp_topk",
    )(q3d, kc, vc, gate3d, r_tab, ovt)


def _rope_tables(pos):
    half = ROPE_DIM // 2
    inv = ROPE_THETA ** (-jnp.arange(half, dtype=F32) / half)
    ang = pos.astype(F32)[:, None] * inv[None, :]
    cos, sin = jnp.cos(ang), jnp.sin(ang)
    ones = jnp.ones((pos.shape[0], HEAD_DIM - ROPE_DIM), F32)
    zeros = jnp.zeros((pos.shape[0], HEAD_DIM - ROPE_DIM), F32)
    zh = jnp.zeros_like(sin)
    c64 = jnp.concatenate([cos, cos, ones], axis=1)
    sa64 = jnp.concatenate([-sin, zh, zeros], axis=1)
    sb64 = jnp.concatenate([zh, sin, zeros], axis=1)
    return tuple(jnp.concatenate([t, t], axis=1) for t in (c64, sa64, sb64))


def _even_in_weight(w):
    o = np.cumsum([0, 512, 512, 512, 512, 128, 128, 128, 128, 128, 128, 24])
    qa, ka, va, qn, kc, vc, ksl, vsl, kw, vw, g = (np.arange(o[i], o[i + 1]) for i in range(11))
    span = lambda a: w[:, int(a[0]):int(a[-1]) + 1]
    qscale = HEAD_DIM ** -0.5 * LOG2E
    qn_heads = [span(qn[(gi * NSA_HPG + j) * HEAD_DIM:(gi * NSA_HPG + j + 1) * HEAD_DIM]) * qscale
                for j in range(NSA_HPG) for gi in range(NSA_KV_HEADS)]
    parts = [span(qa) * qscale] + qn_heads + [span(a) for a in (ka, ksl, kw, va, vsl, vw, kc, vc, g)]
    parts.append(jnp.zeros((w.shape[0], LANES - g.shape[0]), w.dtype))
    return jnp.concatenate(parts, axis=1).astype(BF16)


def _even_out_weight(w):
    heads = [w[512 + (gi * NSA_HPG + j) * HEAD_DIM:512 + (gi * NSA_HPG + j + 1) * HEAD_DIM]
             for j in range(NSA_HPG) for gi in range(NSA_KV_HEADS)]
    return w[:512].astype(BF16), jnp.concatenate(heads, axis=0).astype(BF16)


def _odd_in_weight(w):
    scale = np.ones(w.shape[1], np.float32)
    scale[:D_MODEL] = HEAD_DIM ** -0.5 * LOG2E
    wp = w * jnp.asarray(scale)[None, :]
    return jnp.pad(wp, ((0, 0), (0, LANES - FOX_HEADS))).astype(BF16)


def _compress_weights(pe, w1, b1, w2):
    eye = jnp.eye(NSA_KV_HEADS, dtype=F32)
    w1r = w1.reshape(2, 2, NSA_CMP_STRIDE, HEAD_DIM, NSA_CMP_HIDDEN)
    w1x = jnp.einsum("jardc,gh->jargdhc", w1r, eye)
    w1x = w1x.reshape(2, 2, NSA_CMP_STRIDE * LANES, NSA_KV_HEADS * NSA_CMP_HIDDEN)
    pe_r = pe.reshape(2, 2, NSA_CMP_STRIDE, 1, HEAD_DIM)
    pe_x = jnp.broadcast_to(pe_r, (2, 2, NSA_CMP_STRIDE, NSA_KV_HEADS, HEAD_DIM)).reshape(2, 2, 1, NSA_CMP_STRIDE * LANES)
    pe_x = jnp.broadcast_to(pe_x, (2, 2, 8, NSA_CMP_STRIDE * LANES))
    b1x = jnp.tile(b1, (1, NSA_KV_HEADS)).reshape(2, 1, NSA_KV_HEADS * NSA_CMP_HIDDEN)
    w2x = jnp.einsum("jcd,gh->jgchd", w2, eye).reshape(2, NSA_KV_HEADS * NSA_CMP_HIDDEN, LANES)
    return (w1x[:, 0].astype(BF16), w1x[:, 1].astype(BF16), pe_x.astype(BF16), b1x.astype(F32), w2x.astype(BF16))


def _nsa_constants(seq):
    ncp = seq // NSA_CMP_STRIDE
    nsel = seq // NSA_SEL_LEN
    n = np.arange(ncp)[:, None]
    j = np.arange(LANES)[None, :]
    cs, ce = n * NSA_CMP_STRIDE, n * NSA_CMP_STRIDE + NSA_CMP_LEN - 1
    ss = j * NSA_SEL_LEN
    ov = ((cs <= ss + NSA_SEL_LEN - 1) & (ce >= ss) & (j < nsel) & (n < ncp - 1)).astype(np.float32)
    tok = np.arange(seq)
    eneg = -MASK_BIG * (tok[:, None] // NSA_SEL_LEN == (np.arange(LANES) % HEAD_DIM)[None, :]).astype(np.float32)
    r = np.zeros((NSA_HPG * NSA_N_BRANCH, LANES, LANES), np.float32)
    for jc in range(NSA_HPG):
        for br in range(NSA_N_BRANCH):
            for ln in range(LANES):
                head = (ln // HEAD_DIM) * NSA_HPG + jc
                r[jc * NSA_N_BRANCH + br, head * NSA_N_BRANCH + br, ln] = 1.0
    return jnp.asarray(ov.T.copy(), BF16), jnp.asarray(eneg, BF16), jnp.asarray(r, BF16), nsel


def _fox_query_pieces(c_pieces):
    pieces = jnp.transpose(c_pieces[..., :FOX_HEADS], (1, 3, 0, 2))
    return jnp.pad(pieces, ((0, 0), (0, 0), (0, FOX_AUG_ROWS - FOX_ONE_LANE - 3), (0, 0)))


def kernel(x, ln_gain, ln_bias, mlp_w_up, mlp_w_down, w_in_even, w_out_even, diff_lambda, diff_subln,
           nsa_pe, nsa_cmp_w1, nsa_cmp_b1, nsa_cmp_w2, w_in_odd, fox_f_bias, w_out_odd):
    b, s, d = x.shape
    n = b * s
    ncp = s // NSA_CMP_STRIDE
    xf = x.reshape(n, d)
    tables = _rope_tables(jnp.arange(s))
    tables_c = _rope_tables(jnp.arange(ncp) * NSA_CMP_STRIDE + NSA_CMP_LEN - 1)
    ovt, eneg, r_tab, nsel = _nsa_constants(s)
    row = lambda a: a.reshape(1, -1).astype(F32)
    t3 = lambda a: jnp.transpose(a.reshape(b, s, -1), (0, 2, 1))
    full2 = lambda a: pl.BlockSpec(a.shape, lambda bi, ci: (0, 0))

    for layer in range(DEPTH):
        li = layer // 2
        if layer % 2 == 0:
            q, k, vv, kci, vci, gate = _proj(xf, _even_in_weight(w_in_even[li]), tables, EVEN_SEGS, EVEN_DTYPES, s)
            q3, qt3, k3, vt3, gate3 = q.reshape(b, s, -1), t3(q), k.reshape(b, s, -1), t3(vv), gate.reshape(b, s, LANES)
            lam_init = 0.8 - 0.6 * math.exp(-0.3 * layer)
            lam_p, subln = diff_lambda[li].astype(F32), row(diff_subln[li])
            o_a = _flash_t("diff", qt3, 0, k3, 0, vt3, 0, DIFF_HEADS, True, BF16,
                           extras=((lam_p, full2(lam_p)), (subln, full2(subln))), lam_init=lam_init)
            w1a, w1b, pe_x, b1x, w2x = _compress_weights(nsa_pe[li], nsa_cmp_w1[li], nsa_cmp_b1[li], nsa_cmp_w2[li])
            kc, vc = _nsa_compress(kci.reshape(b, ncp, -1), vci.reshape(b, ncp, -1), w1a, w1b, pe_x, b1x, w2x, tables_c)
            o_c, notsel_t = _nsa_cmp(q3, 1, kc, vc, gate3, r_tab, ovt, nsel)
            gate_spec = pl.BlockSpec((1, s, LANES), lambda bi, ci: (bi, 0, 0))
            prev_spec = pl.BlockSpec((1, s, LANES), lambda bi, ci: (bi, 0, ci))
            r_spec = lambda br: pl.BlockSpec((1, LANES, LANES), lambda bi, ci: (ci * NSA_N_BRANCH + br, 0, 0))
            nsel_spec = pl.BlockSpec((1, NSA_KV_HEADS, HEAD_DIM, s), lambda bi, ci: (bi, 0, 0, 0))
            o_cs = _flash_t("sel", qt3, 4, k3, 4, vt3, 4, NSA_HPG, False, F32,
                            extras=((eneg, full2(eneg)), (notsel_t, nsel_spec), (gate3, gate_spec), (r_tab, r_spec(1)),
                                    (o_c, prev_spec)))
            o_b = _flash_t("win", qt3, 4, k3, 5, vt3, 5, NSA_HPG, False, BF16,
                           extras=((gate3, gate_spec), (r_tab, r_spec(2)), (o_cs, prev_spec)), window=NSA_WINDOW)
            wo_a, wo_b = _even_out_weight(w_out_even[li])
            mix_in, mix_w = [o_a.reshape(n, -1), o_b.reshape(n, -1)], [wo_a, wo_b]
        else:
            q, k, v, fl = _proj(xf, _odd_in_weight(w_in_odd[li]), None, ODD_SEGS, ODD_DTYPES, s)
            bias = jnp.pad(fox_f_bias[li].astype(F32), (0, LANES - FOX_HEADS)).reshape(1, LANES)
            c_pieces, kaug = _fox_cumsum(fl.reshape(b, s, LANES), bias)
            qaug_t = _fox_query_pieces(c_pieces)
            kaug_spec = pl.BlockSpec((1, s, LANES), lambda bi, ci: (bi, 0, 0))
            qaug_spec = pl.BlockSpec((1, 2, FOX_AUG_ROWS - FOX_ONE_LANE, s), lambda bi, ci: (bi, ci, 0, 0))
            o_f = _flash_t("fox", t3(q), 0, k.reshape(b, s, -1), 0, t3(v), 0, FOX_HEADS // 2, True, BF16,
                           extras=((kaug, kaug_spec), (qaug_t, qaug_spec)))
            mix_in, mix_w = [o_f.reshape(n, -1)], [w_out_odd[li].astype(BF16)]
        xf = _mix_mlp(mix_in, mix_w, xf, mlp_w_up[layer].astype(BF16), mlp_w_down[layer].astype(BF16),
                      row(ln_gain[layer, 0]), row(ln_bias[layer, 0]), row(ln_gain[layer, 1]), row(ln_bias[layer, 1]))
    return xf.reshape(b, s, d)
```

```python
import functools
import math

import numpy as np
import jax
import jax.numpy as jnp
from jax import lax
from jax.experimental import pallas as pl
from jax.experimental.pallas import tpu as pltpu

F32 = jnp.float32
BF16 = jnp.bfloat16

D_MODEL = 1024
DEPTH = 4
HEAD_DIM = 64
ROPE_DIM = HEAD_DIM // 4
ROPE_THETA = 500000.0
LN_EPS = 1e-5
DIFF_HEADS = 4
NSA_HEADS = 8
NSA_KV_HEADS = 2
NSA_HPG = NSA_HEADS // NSA_KV_HEADS
NSA_CMP_LEN = 32
NSA_CMP_STRIDE = 16
NSA_CMP_HIDDEN = 256
NSA_SEL_LEN = 64
NSA_TOPK = 8
NSA_WINDOW = 512
NSA_N_BRANCH = 3
NSA_BIG = 1e4
FOX_HEADS = D_MODEL // HEAD_DIM
D_FF = 4 * D_MODEL
ALPHA = (2 * DEPTH) ** 0.25

LANES = 128
NEG = -1e30
VMEM_LIMIT = 56 * 1024 * 1024
LOG2E = 1.4426950408889634
MASK_BIG = 30000.0
FOX_ONE_LANE = 3 * 16
FOX_AUG_ROWS = HEAD_DIM
ONES_ROWS = 16
SCORE_LOOKAHEAD = 8

EVEN_SEGS = ((0, 1024, True), (1024, 768, True), (1792, 768, False),
             (2560, 128, False), (2688, 128, False), (2816, 128, False))
EVEN_DTYPES = (BF16, BF16, BF16, BF16, BF16, F32)
ODD_SEGS = ((0, D_MODEL, False), (D_MODEL, D_MODEL, False), (2 * D_MODEL, D_MODEL, False),
            (3 * D_MODEL, 128, False))
ODD_DTYPES = (BF16, BF16, BF16, F32)


def _dot(a, b):
    return jnp.dot(a, b, preferred_element_type=F32)


def _dot_nt(a, b):
    return lax.dot_general(a, b, (((1,), (1,)), ((), ())), preferred_element_type=F32)


def _split3(x):
    p0 = x.astype(BF16)
    r1 = x - p0.astype(F32)
    p1 = r1.astype(BF16)
    p2 = (r1 - p1.astype(F32)).astype(BF16)
    return p0, p1, p2


def _rope(h, cos, sa, sb):
    return h * cos + pltpu.roll(h, LANES - ROPE_DIM // 2, 1) * sa + pltpu.roll(h, ROPE_DIM // 2, 1) * sb


def _layer_norm(y, g, b):
    mu = jnp.mean(y, axis=-1, keepdims=True)
    yc = y - mu
    var = jnp.mean(yc * yc, axis=-1, keepdims=True)
    return yc * lax.rsqrt(var + LN_EPS) * g + b


def _params(*sem):
    return pltpu.CompilerParams(dimension_semantics=sem, vmem_limit_bytes=VMEM_LIMIT)


def _proj_kernel(*refs, segs, has_rope):
    x_ref, w_ref = refs[0], refs[1]
    pos = 2
    if has_rope:
        cos_ref, sa_ref, sb_ref = refs[2:5]
        pos = 5
    out_refs = refs[pos:]
    xb = x_ref[...].astype(BF16)
    for o_ref, (c0, width, rope) in zip(out_refs, segs):
        for cc in range(0, width, 512):
            w = min(512, width - cc)
            h = _dot(xb, w_ref[:, c0 + cc:c0 + cc + w])
            if rope:
                cos, sa, sb = cos_ref[...], sa_ref[...], sb_ref[...]
                for t in range(0, w, LANES):
                    o_ref[:, cc + t:cc + t + LANES] = _rope(h[:, t:t + LANES], cos, sa, sb).astype(o_ref.dtype)
            else:
                o_ref[:, cc:cc + w] = h.astype(o_ref.dtype)


def _proj(x2d, w, tables, segs, dtypes, seq, tm=512):
    n, d = x2d.shape
    m = w.shape[1]
    ns = seq // tm
    has_rope = tables is not None
    in_specs = [pl.BlockSpec((tm, d), lambda i: (i, 0)), pl.BlockSpec((d, m), lambda i: (0, 0))]
    args = [x2d, w]
    if has_rope:
        in_specs += [pl.BlockSpec((tm, LANES), lambda i: (i % ns, 0))] * 3
        args += list(tables)
    return pl.pallas_call(
        functools.partial(_proj_kernel, segs=segs, has_rope=has_rope),
        grid=(n // tm,),
        in_specs=in_specs,
        out_specs=[pl.BlockSpec((tm, s[1]), lambda i: (i, 0)) for s in segs],
        out_shape=[jax.ShapeDtypeStruct((n, s[1]), dt) for s, dt in zip(segs, dtypes)],
        compiler_params=_params("parallel"),
        name="in_proj",
    )(*args)


def _mix_mlp_kernel(*refs, n_in, ff_chunk):
    o_refs = refs[:n_in]
    w_refs = refs[n_in:2 * n_in]
    x_ref, wu_ref, wd_ref, g1_ref, b1_ref, g2_ref, b2_ref, y_ref = refs[2 * n_in:]
    mix = _dot(o_refs[0][...], w_refs[0][...])
    for o_ref, w_ref in zip(o_refs[1:], w_refs[1:]):
        mix = mix + _dot(o_ref[...], w_ref[...])
    x = _layer_norm(ALPHA * x_ref[...] + mix, g1_ref[...], b1_ref[...])
    xb = x.astype(BF16)
    acc = None
    for c0 in range(0, wu_ref.shape[1], ff_chunk):
        h = jnp.maximum(_dot(xb, wu_ref[:, c0:c0 + ff_chunk]), 0.0)
        part = _dot((h * h).astype(BF16), wd_ref[c0:c0 + ff_chunk, :])
        acc = part if acc is None else acc + part
    y_ref[...] = _layer_norm(ALPHA * x + acc, g2_ref[...], b2_ref[...])


def _mix_mlp(os_, ws, x2d, wu, wd, g1, b1, g2, b2, tm=512, ff_chunk=1024):
    n, d = x2d.shape
    resident = lambda a: pl.BlockSpec(a.shape, lambda i: (0, 0), pipeline_mode=pl.Buffered(1))
    rowvec = pl.BlockSpec((1, d), lambda i: (0, 0))
    in_specs = [pl.BlockSpec((tm, o.shape[1]), lambda i: (i, 0)) for o in os_]
    in_specs += [resident(w) for w in ws]
    in_specs += [pl.BlockSpec((tm, d), lambda i: (i, 0)), resident(wu), resident(wd), rowvec, rowvec, rowvec, rowvec]
    return pl.pallas_call(
        functools.partial(_mix_mlp_kernel, n_in=len(os_), ff_chunk=ff_chunk),
        grid=(n // tm,),
        in_specs=in_specs,
        out_specs=pl.BlockSpec((tm, d), lambda i: (i, 0)),
        out_shape=jax.ShapeDtypeStruct((n, d), F32),
        compiler_params=_params("parallel"),
        name="mix_mlp_ln",
    )(*os_, *ws, x2d, wu, wd, g1, b1, g2, b2)


def _fox_cumsum_kernel(f_ref, bias_ref, tri_ref, scat_ref, onesrow_ref, cp_ref, kaug_ref, *, blk):
    tri = tri_ref[...]
    carry = jnp.zeros((1, LANES), F32)
    for r0 in range(0, f_ref.shape[1], blk):
        z = f_ref[0, r0:r0 + blk, :] + bias_ref[...]
        lf = jnp.minimum(z, 0.0) - jnp.log1p(jnp.exp(-jnp.abs(z)))
        p0, p1, p2 = _split3(lf)
        cs = _dot(tri, p0) + _dot(tri, p1) + _dot(tri, p2) + carry
        pieces = _split3(cs * LOG2E)
        kaug = onesrow_ref[...]
        for i, piece in enumerate(pieces):
            cp_ref[i, 0, r0:r0 + blk, :] = piece
            kaug = kaug + _dot(piece, scat_ref[i])
        kaug_ref[0, r0:r0 + blk, :] = kaug.astype(BF16)
        carry = cs[blk - 1:blk, :]


def _fox_cumsum(f3d, bias, blk=256):
    b, s, _ = f3d.shape
    blk = min(blk, s)
    tri = jnp.asarray(np.tril(np.ones((blk, blk), np.float32)), BF16)
    scat = np.zeros((3, LANES, LANES), np.float32)
    onesrow = np.zeros((1, LANES), np.float32)
    for half in (0, HEAD_DIM):
        for head in range(FOX_HEADS):
            for i in range(3):
                scat[i, head, half + 3 * head + i] = 1.0
        onesrow[0, half + FOX_ONE_LANE:half + FOX_ONE_LANE + 3] = 1.0
    return pl.pallas_call(
        functools.partial(_fox_cumsum_kernel, blk=blk),
        grid=(b,),
        in_specs=[pl.BlockSpec((1, s, LANES), lambda i: (i, 0, 0)),
                  pl.BlockSpec((1, LANES), lambda i: (0, 0)),
                  pl.BlockSpec((blk, blk), lambda i: (0, 0)),
                  pl.BlockSpec((3, LANES, LANES), lambda i: (0, 0, 0)),
                  pl.BlockSpec((1, LANES), lambda i: (0, 0))],
        out_specs=[pl.BlockSpec((3, 1, s, LANES), lambda i: (0, i, 0, 0)),
                   pl.BlockSpec((1, s, LANES), lambda i: (i, 0, 0))],
        out_shape=[jax.ShapeDtypeStruct((3, b, s, LANES), BF16),
                   jax.ShapeDtypeStruct((b, s, LANES), BF16)],
        compiler_params=_params("parallel"),
        name="fox_cumsum",
    )(f3d, bias, tri, jnp.asarray(scat, BF16), jnp.asarray(onesrow))


def _gate_lanes(gate_logits, r):
    sig = jax.nn.sigmoid(gate_logits)
    hi = sig.astype(BF16)
    lo = (sig - hi.astype(F32)).astype(BF16)
    return _dot(hi, r) + _dot(lo, r)


def _flash_t_kernel(*refs, mode, tk, qc, window, lam_init):
    it = iter(refs)
    qt_ref, k_ref, vt_ref = next(it), next(it), next(it)
    if mode in ("fox", "sel"):
        kaug_ref, qaugt_ref = next(it), next(it)
    if mode == "diff":
        lam_ref, subln_ref = next(it), next(it)
    if mode in ("sel", "win"):
        gate_ref, r_ref, prev_ref = next(it), next(it), next(it)
    eye_ref = next(it)
    o_ref, qa_ref, m_ref, acc_ref = next(it), next(it), next(it), next(it)

    seq = k_ref.shape[1]
    n_t, n_q = seq // tk, seq // qc
    vrows = acc_ref.shape[2] - ONES_ROWS

    qt = qt_ref[0]
    if mode == "fox":
        row = lax.broadcasted_iota(jnp.int32, (FOX_ONE_LANE, seq), 0)
        augs = []
        for h in (0, 1):
            own = (row - 3 * (2 * pl.program_id(1) + h)).astype(jnp.uint32) < 3
            augs.append(jnp.concatenate([jnp.where(own, -1.0, 0.0).astype(BF16), qaugt_ref[0, h]], axis=0))
    elif mode == "sel":
        augs = (qaugt_ref[0, 0], qaugt_ref[0, 1])
    else:
        augs = (jnp.zeros((HEAD_DIM, seq), BF16),) * 2
    for qi in range(n_q):
        cs = slice(qi * qc, (qi + 1) * qc)
        qa_ref[0, qi] = jnp.concatenate([qt[0:HEAD_DIM, cs], augs[0][:, cs]], axis=0)
        qa_ref[1, qi] = jnp.concatenate([augs[1][:, cs], qt[HEAD_DIM:LANES, cs]], axis=0)
    m_ref[...] = jnp.full(m_ref.shape, NEG, F32)
    acc_ref[...] = jnp.zeros(acc_ref.shape, F32)

    lo_half_k = lax.broadcasted_iota(jnp.int32, (tk, LANES), 1) < HEAD_DIM
    ones = jnp.ones((ONES_ROWS, tk), BF16)
    n_back = window // tk

    def colmax(x):
        r = x.shape[0]
        while r > 8:
            r //= 2
            x = jnp.maximum(x[0:r], x[r:2 * r])
        return jnp.max(x, axis=0, keepdims=True)

    chains = []
    for d in range(n_t):
        for h in (0, 1):
            for qi in range(n_q):
                if (d <= qi <= d + n_back) if mode == "win" else (qi >= d):
                    chains.append((d, h, qi))

    key_tiles, value_tiles = {}, {}

    def key_tile(d, h):
        if (d, h) not in key_tiles:
            k0 = d * tk
            ka = k_ref[0, k0:k0 + tk, :]
            if mode in ("fox", "sel"):
                kaug = kaug_ref[0, k0:k0 + tk, :] if mode == "fox" else kaug_ref[k0:k0 + tk, :]
                ka = jnp.where(lo_half_k, ka, kaug) if h == 0 else jnp.where(lo_half_k, kaug, ka)
            key_tiles[(d, h)] = ka
        return key_tiles[(d, h)]

    def value_tile(d, h):
        if (d, h) not in value_tiles:
            k0 = d * tk
            vt = vt_ref[0, :, k0:k0 + tk] if mode == "diff" else vt_ref[0, h * HEAD_DIM:(h + 1) * HEAD_DIM, k0:k0 + tk]
            value_tiles[(d, h)] = jnp.concatenate([vt, ones], axis=0)
        return value_tiles[(d, h)]

    def scores(d, h, qi):
        k0 = d * tk
        s = _dot(key_tile(d, h), qa_ref[h, qi])
        if qi == d or (mode == "win" and qi == d + n_back):
            rows = lax.broadcasted_iota(jnp.int32, (tk, qc), 0)
            cols = lax.broadcasted_iota(jnp.int32, (tk, qc), 1)
            dist = (qi * qc - k0) + cols - rows
            if qi == d:
                s = jnp.where(dist >= 0, s, NEG)
            else:
                s = jnp.where(dist < window, s, NEG)
        return s

    pending = [scores(*ch) for ch in chains[:SCORE_LOOKAHEAD]]
    for idx, (d, h, qi) in enumerate(chains):
        s = pending.pop(0)
        m_prev = m_ref[h, qi]
        m_new = jnp.maximum(m_prev, colmax(s))
        p = jnp.exp2((s - m_new).astype(BF16))
        alpha = jnp.exp2(m_prev - m_new)
        acc_ref[h, qi] = alpha * acc_ref[h, qi] + _dot(value_tile(d, h), p)
        m_ref[h, qi] = m_new
        if idx + SCORE_LOOKAHEAD < len(chains):
            pending.append(scores(*chains[idx + SCORE_LOOKAHEAD]))

    if mode == "diff":
        lp = lam_ref[...]
        lam = (jnp.exp(jnp.sum(lp[0:1] * lp[1:2], axis=-1, keepdims=True))
               - jnp.exp(jnp.sum(lp[2:3] * lp[3:4], axis=-1, keepdims=True)) + lam_init)
    eye = eye_ref[...]
    for qi in range(n_q):
        c0 = qi * qc
        outs = [acc_ref[h, qi, 0:vrows, :] / acc_ref[h, qi, vrows:vrows + 1, :] for h in (0, 1)]
        if mode == "diff":
            o_t = outs[0] - lam * outs[1]
            ms = jnp.mean(o_t * o_t, axis=0, keepdims=True)
            o_t = o_t * lax.rsqrt(ms + LN_EPS) * subln_ref[...] * (1.0 - lam_init)
        else:
            o_t = jnp.concatenate(outs, axis=0)
        hi = o_t.astype(BF16)
        if mode in ("fox", "diff"):
            o = _dot_nt(eye, hi)
        else:
            lo = (o_t - hi.astype(F32)).astype(BF16)
            o = _dot_nt(eye, hi) + _dot_nt(eye, lo)
        if mode in ("sel", "win"):
            o = prev_ref[0, c0:c0 + qc, :] + _gate_lanes(gate_ref[0, c0:c0 + qc, :], r_ref[0]) * o
        o_ref[0, c0:c0 + qc, :] = o.astype(o_ref.dtype)


def _flash_t(mode, qt_arr, q_off, k_arr, k_off, vt_arr, v_off, n_chunks, kv_per_chunk, out_dtype,
             extras=(), window=0, lam_init=0.0, tk=256, qc=256):
    b, s, _ = k_arr.shape
    tk, qc = min(tk, s), min(qc, s)
    k_map = (lambda bi, ci: (bi, 0, k_off + ci)) if kv_per_chunk else (lambda bi, ci: (bi, 0, k_off))
    v_map = (lambda bi, ci: (bi, v_off + ci, 0)) if kv_per_chunk else (lambda bi, ci: (bi, v_off, 0))
    eye = jnp.asarray(np.eye(qc, dtype=np.float32), BF16)
    in_specs = [pl.BlockSpec((1, LANES, s), lambda bi, ci: (bi, q_off + ci, 0)),
                pl.BlockSpec((1, s, LANES), k_map),
                pl.BlockSpec((1, LANES, s), v_map)]
    args = [qt_arr, k_arr, vt_arr]
    for arr, spec in extras:
        args.append(arr)
        in_specs.append(spec)
    args.append(eye)
    in_specs.append(pl.BlockSpec((qc, qc), lambda bi, ci: (0, 0)))
    acc_rows = (LANES if mode == "diff" else HEAD_DIM) + ONES_ROWS
    return pl.pallas_call(
        functools.partial(_flash_t_kernel, mode=mode, tk=tk, qc=qc, window=window, lam_init=lam_init),
        grid=(b, n_chunks),
        in_specs=in_specs,
        out_specs=pl.BlockSpec((1, s, LANES), lambda bi, ci: (bi, 0, ci)),
        out_shape=jax.ShapeDtypeStruct((b, s, n_chunks * LANES), out_dtype),
        scratch_shapes=[pltpu.VMEM((2, s // qc, LANES, qc), BF16), pltpu.VMEM((2, s // qc, 1, qc), F32),
                        pltpu.VMEM((2, s // qc, acc_rows, qc), F32)],
        compiler_params=_params("parallel", "parallel"),
        name="flash_" + mode,
    )(*args)


def _nsa_compress_kernel(tk_ref, tv_ref, w1a_ref, w1b_ref, pe_ref, b1_ref, w2_ref,
                         cos_ref, sa_ref, sb_ref, kc_ref, vc_ref):
    ncp = tk_ref.shape[1]
    for j, (t_ref, o_ref) in enumerate(((tk_ref, kc_ref), (tv_ref, vc_ref))):
        t = t_ref[0]
        u = _dot(t, w1a_ref[j])
        v = _dot(t, w1b_ref[j])
        const = _dot(pe_ref[j, 0], w1a_ref[j]) + _dot(pe_ref[j, 1], w1b_ref[j])
        pre = u + pltpu.roll(v, ncp - 1, 0) + const[0:1] + b1_ref[j]
        h = jax.nn.gelu(pre, approximate=True)
        out = _dot(h.astype(BF16), w2_ref[j])
        if j == 0:
            out = _rope(out, cos_ref[...], sa_ref[...], sb_ref[...])
        o_ref[0] = out.astype(o_ref.dtype)


def _nsa_compress(tkc, tvc, w1a, w1b, pe, b1, w2, tables_c):
    b, ncp, width = tkc.shape
    full = lambda a: pl.BlockSpec(a.shape, lambda i: (0,) * a.ndim)
    return pl.pallas_call(
        _nsa_compress_kernel,
        grid=(b,),
        in_specs=[pl.BlockSpec((1, ncp, width), lambda i: (i, 0, 0)),
                  pl.BlockSpec((1, ncp, width), lambda i: (i, 0, 0)),
                  full(w1a), full(w1b), full(pe), full(b1), full(w2)] + [full(t) for t in tables_c],
        out_specs=[pl.BlockSpec((1, ncp, LANES), lambda i: (i, 0, 0))] * 2,
        out_shape=[jax.ShapeDtypeStruct((b, ncp, LANES), BF16)] * 2,
        compiler_params=_params("parallel"),
        name="nsa_compress",
    )(tkc, tvc, w1a, w1b, pe, b1, w2, *tables_c)


def _nsa_cmp_kernel(q_ref, kc_ref, vc_ref, gate_ref, r_ref, ovt_ref, oc_ref, nsel_ref, *, tq, nsel, topk):
    qi = pl.program_id(1)
    kc = kc_ref[0]
    vc = vc_ref[0]
    ncp = kc.shape[0]
    ovt = ovt_ref[...]
    nsr = -(-nsel // 8) * 8
    lane = lax.broadcasted_iota(jnp.int32, (tq, LANES), 1)
    lo_half = lane < HEAD_DIM
    tpos_c = qi * tq + lax.broadcasted_iota(jnp.int32, (tq, ncp), 0)
    ncol = lax.broadcasted_iota(jnp.int32, (tq, ncp), 1)
    valid = (NSA_CMP_STRIDE * ncol + NSA_CMP_LEN - 1) <= tpos_c
    blk = lax.broadcasted_iota(jnp.int32, (nsr, tq), 0)
    cur = (qi * tq + lax.broadcasted_iota(jnp.int32, (nsr, tq), 1)) // NSA_SEL_LEN
    gates = jax.nn.sigmoid(gate_ref[0])
    g_hi = gates.astype(BF16)
    g_lo = (gates - g_hi.astype(F32)).astype(BF16)

    o_chunks = [None] * NSA_HPG
    gap = jnp.zeros((HEAD_DIM - nsr, tq), F32)
    logits = {}
    for g in range(NSA_KV_HEADS):
        in_half = lo_half if g == 0 else jnp.logical_not(lo_half)
        for j in range(NSA_HPG):
            qc = q_ref[0, :, j * LANES:(j + 1) * LANES]
            logits[g, j] = _dot_nt(jnp.where(in_half, qc, jnp.zeros_like(qc)), kc)
    for g in range(NSA_KV_HEADS):
        psum = jnp.zeros((tq, ncp), F32)
        for j in range(NSA_HPG):
            s = jnp.where(valid, logits[g, j], NEG)
            e = jnp.exp2(s - jnp.max(s, axis=-1, keepdims=True))
            p = jnp.where(valid, e / jnp.sum(e, axis=-1, keepdims=True), 0.0)
            psum = psum + p
            o = _dot(p.astype(BF16), vc)
            o_chunks[j] = o if g == 0 else jnp.where(lo_half, o_chunks[j], o)
        p_hi = psum.astype(BF16)
        p_lo = (psum - p_hi.astype(F32)).astype(BF16)
        imp = (_dot_nt(ovt, p_hi) + _dot_nt(ovt, p_lo))[0:nsr]
        imp = jnp.where(blk > cur, -NSA_BIG, imp)
        imp = jnp.where(blk == cur - 1, NSA_BIG, imp)
        imp = jnp.where(blk == cur, NSA_BIG, imp)
        imp = jnp.where(blk == 0, NSA_BIG, imp)
        imp = jnp.where(blk >= nsel, -3.0 * NSA_BIG, imp)
        rank = jnp.zeros((nsr, tq), jnp.int32)
        for jp in range(nsel):
            other = imp[jp:jp + 1, :]
            ahead = jnp.where(other > imp, 1, jnp.where(other == imp, jnp.where(blk > jp, 1, 0), 0))
            rank = rank + ahead
        nsel_ref[0, g] = jnp.concatenate([jnp.where(rank < topk, 0.0, 1.0), gap], axis=0).astype(nsel_ref.dtype)
    for j in range(NSA_HPG):
        r = r_ref[j * NSA_N_BRANCH]
        oc_ref[0, :, j * LANES:(j + 1) * LANES] = (_dot(g_hi, r) + _dot(g_lo, r)) * o_chunks[j]


def _nsa_cmp(q3d, q_blk, kc, vc, gate3d, r_tab, ovt, nsel, tq=256):
    b, s, _ = q3d.shape
    tq = min(tq, s)
    ncp = kc.shape[1]
    width = NSA_HPG * LANES
    full = lambda a: pl.BlockSpec(a.shape, lambda bi, qi: (0,) * a.ndim)
    return pl.pallas_call(
        functools.partial(_nsa_cmp_kernel, tq=tq, nsel=nsel, topk=min(NSA_TOPK, nsel)),
        grid=(b, s // tq),
        in_specs=[pl.BlockSpec((1, tq, width), lambda bi, qi: (bi, qi, q_blk)),
                  pl.BlockSpec((1, ncp, LANES), lambda bi, qi: (bi, 0, 0)),
                  pl.BlockSpec((1, ncp, LANES), lambda bi, qi: (bi, 0, 0)),
                  pl.BlockSpec((1, tq, LANES), lambda bi, qi: (bi, qi, 0)),
                  full(r_tab), full(ovt)],
        out_specs=[pl.BlockSpec((1, tq, width), lambda bi, qi: (bi, qi, 0)),
                   pl.BlockSpec((1, NSA_KV_HEADS, HEAD_DIM, tq), lambda bi, qi: (bi, 0, 0, qi))],
        out_shape=[jax.ShapeDtypeStruct((b, s, width), F32),
                   jax.ShapeDtypeStruct((b, NSA_KV_HEADS, HEAD_DIM, s), BF16)],
        compiler_params=_params("parallel", "arbitrary"),
        name="nsa_cmp_topk",
    )(q3d, kc, vc, gate3d, r_tab, ovt)


def _rope_tables(pos):
    half = ROPE_DIM // 2
    inv = ROPE_THETA ** (-jnp.arange(half, dtype=F32) / half)
    ang = pos.astype(F32)[:, None] * inv[None, :]
    cos, sin = jnp.cos(ang), jnp.sin(ang)
    ones = jnp.ones((pos.shape[0], HEAD_DIM - ROPE_DIM), F32)
    zeros = jnp.zeros((pos.shape[0], HEAD_DIM - ROPE_DIM), F32)
    zh = jnp.zeros_like(sin)
    c64 = jnp.concatenate([cos, cos, ones], axis=1)
    sa64 = jnp.concatenate([-sin, zh, zeros], axis=1)
    sb64 = jnp.concatenate([zh, sin, zeros], axis=1)
    return tuple(jnp.concatenate([t, t], axis=1) for t in (c64, sa64, sb64))


def _even_in_weight(w):
    o = np.cumsum([0, 512, 512, 512, 512, 128, 128, 128, 128, 128, 128, 24])
    qa, ka, va, qn, kc, vc, ksl, vsl, kw, vw, g = (np.arange(o[i], o[i + 1]) for i in range(11))
    span = lambda a: w[:, int(a[0]):int(a[-1]) + 1]
    qscale = HEAD_DIM ** -0.5 * LOG2E
    qn_heads = [span(qn[(gi * NSA_HPG + j) * HEAD_DIM:(gi * NSA_HPG + j + 1) * HEAD_DIM]) * qscale
                for j in range(NSA_HPG) for gi in range(NSA_KV_HEADS)]
    parts = [span(qa) * qscale] + qn_heads + [span(a) for a in (ka, ksl, kw, va, vsl, vw, kc, vc, g)]
    parts.append(jnp.zeros((w.shape[0], LANES - g.shape[0]), w.dtype))
    return jnp.concatenate(parts, axis=1).astype(BF16)


def _even_out_weight(w):
    heads = [w[512 + (gi * NSA_HPG + j) * HEAD_DIM:512 + (gi * NSA_HPG + j + 1) * HEAD_DIM]
             for j in range(NSA_HPG) for gi in range(NSA_KV_HEADS)]
    return w[:512].astype(BF16), jnp.concatenate(heads, axis=0).astype(BF16)


def _odd_in_weight(w):
    scale = np.ones(w.shape[1], np.float32)
    scale[:D_MODEL] = HEAD_DIM ** -0.5 * LOG2E
    wp = w * jnp.asarray(scale)[None, :]
    return jnp.pad(wp, ((0, 0), (0, LANES - FOX_HEADS))).astype(BF16)


def _compress_weights(pe, w1, b1, w2):
    eye = jnp.eye(NSA_KV_HEADS, dtype=F32)
    w1r = w1.reshape(2, 2, NSA_CMP_STRIDE, HEAD_DIM, NSA_CMP_HIDDEN)
    w1x = jnp.einsum("jardc,gh->jargdhc", w1r, eye)
    w1x = w1x.reshape(2, 2, NSA_CMP_STRIDE * LANES, NSA_KV_HEADS * NSA_CMP_HIDDEN)
    pe_r = pe.reshape(2, 2, NSA_CMP_STRIDE, 1, HEAD_DIM)
    pe_x = jnp.broadcast_to(pe_r, (2, 2, NSA_CMP_STRIDE, NSA_KV_HEADS, HEAD_DIM)).reshape(2, 2, 1, NSA_CMP_STRIDE * LANES)
    pe_x = jnp.broadcast_to(pe_x, (2, 2, 8, NSA_CMP_STRIDE * LANES))
    b1x = jnp.tile(b1, (1, NSA_KV_HEADS)).reshape(2, 1, NSA_KV_HEADS * NSA_CMP_HIDDEN)
    w2x = jnp.einsum("jcd,gh->jgchd", w2, eye).reshape(2, NSA_KV_HEADS * NSA_CMP_HIDDEN, LANES)
    return (w1x[:, 0].astype(BF16), w1x[:, 1].astype(BF16), pe_x.astype(BF16), b1x.astype(F32), w2x.astype(BF16))


def _nsa_constants(seq):
    ncp = seq // NSA_CMP_STRIDE
    nsel = seq // NSA_SEL_LEN
    n = np.arange(ncp)[:, None]
    j = np.arange(LANES)[None, :]
    cs, ce = n * NSA_CMP_STRIDE, n * NSA_CMP_STRIDE + NSA_CMP_LEN - 1
    ss = j * NSA_SEL_LEN
    ov = ((cs <= ss + NSA_SEL_LEN - 1) & (ce >= ss) & (j < nsel) & (n < ncp - 1)).astype(np.float32)
    tok = np.arange(seq)
    eneg = -MASK_BIG * (tok[:, None] // NSA_SEL_LEN == (np.arange(LANES) % HEAD_DIM)[None, :]).astype(np.float32)
    r = np.zeros((NSA_HPG * NSA_N_BRANCH, LANES, LANES), np.float32)
    for jc in range(NSA_HPG):
        for br in range(NSA_N_BRANCH):
            for ln in range(LANES):
                head = (ln // HEAD_DIM) * NSA_HPG + jc
                r[jc * NSA_N_BRANCH + br, head * NSA_N_BRANCH + br, ln] = 1.0
    return jnp.asarray(ov.T.copy(), BF16), jnp.asarray(eneg, BF16), jnp.asarray(r, BF16), nsel


def _fox_query_pieces(c_pieces):
    pieces = jnp.transpose(c_pieces[..., :FOX_HEADS], (1, 3, 0, 2))
    return jnp.pad(pieces, ((0, 0), (0, 0), (0, FOX_AUG_ROWS - FOX_ONE_LANE - 3), (0, 0)))


def kernel(x, ln_gain, ln_bias, mlp_w_up, mlp_w_down, w_in_even, w_out_even, diff_lambda, diff_subln,
           nsa_pe, nsa_cmp_w1, nsa_cmp_b1, nsa_cmp_w2, w_in_odd, fox_f_bias, w_out_odd):
    b, s, d = x.shape
    n = b * s
    ncp = s // NSA_CMP_STRIDE
    xf = x.reshape(n, d)
    tables = _rope_tables(jnp.arange(s))
    tables_c = _rope_tables(jnp.arange(ncp) * NSA_CMP_STRIDE + NSA_CMP_LEN - 1)
    ovt, eneg, r_tab, nsel = _nsa_constants(s)
    row = lambda a: a.reshape(1, -1).astype(F32)
    t3 = lambda a: jnp.transpose(a.reshape(b, s, -1), (0, 2, 1))
    full2 = lambda a: pl.BlockSpec(a.shape, lambda bi, ci: (0, 0))

    for layer in range(DEPTH):
        li = layer // 2
        if layer % 2 == 0:
            q, k, vv, kci, vci, gate = _proj(xf, _even_in_weight(w_in_even[li]), tables, EVEN_SEGS, EVEN_DTYPES, s)
            q3, qt3, k3, vt3, gate3 = q.reshape(b, s, -1), t3(q), k.reshape(b, s, -1), t3(vv), gate.reshape(b, s, LANES)
            lam_init = 0.8 - 0.6 * math.exp(-0.3 * layer)
            lam_p = diff_lambda[li].astype(F32)
            subln = jnp.broadcast_to(diff_subln[li].astype(F32)[:, None], (LANES, min(256, s)))
            o_a = _flash_t("diff", qt3, 0, k3, 0, vt3, 0, DIFF_HEADS, True, BF16,
                           extras=((lam_p, full2(lam_p)), (subln, full2(subln))), lam_init=lam_init)
            w1a, w1b, pe_x, b1x, w2x = _compress_weights(nsa_pe[li], nsa_cmp_w1[li], nsa_cmp_b1[li], nsa_cmp_w2[li])
            kc, vc = _nsa_compress(kci.reshape(b, ncp, -1), vci.reshape(b, ncp, -1), w1a, w1b, pe_x, b1x, w2x, tables_c)
            o_c, notsel_t = _nsa_cmp(q3, 1, kc, vc, gate3, r_tab, ovt, nsel)
            gate_spec = pl.BlockSpec((1, s, LANES), lambda bi, ci: (bi, 0, 0))
            prev_spec = pl.BlockSpec((1, s, LANES), lambda bi, ci: (bi, 0, ci))
            r_spec = lambda br: pl.BlockSpec((1, LANES, LANES), lambda bi, ci: (ci * NSA_N_BRANCH + br, 0, 0))
            nsel_spec = pl.BlockSpec((1, NSA_KV_HEADS, HEAD_DIM, s), lambda bi, ci: (bi, 0, 0, 0))
            o_cs = _flash_t("sel", qt3, 4, k3, 4, vt3, 4, NSA_HPG, False, F32,
                            extras=((eneg, full2(eneg)), (notsel_t, nsel_spec), (gate3, gate_spec), (r_tab, r_spec(1)),
                                    (o_c, prev_spec)))
            o_b = _flash_t("win", qt3, 4, k3, 5, vt3, 5, NSA_HPG, False, BF16,
                           extras=((gate3, gate_spec), (r_tab, r_spec(2)), (o_cs, prev_spec)), window=NSA_WINDOW)
            wo_a, wo_b = _even_out_weight(w_out_even[li])
            mix_in, mix_w = [o_a.reshape(n, -1), o_b.reshape(n, -1)], [wo_a, wo_b]
        else:
            q, k, v, fl = _proj(xf, _odd_in_weight(w_in_odd[li]), None, ODD_SEGS, ODD_DTYPES, s)
            bias = jnp.pad(fox_f_bias[li].astype(F32), (0, LANES - FOX_HEADS)).reshape(1, LANES)
            c_pieces, kaug = _fox_cumsum(fl.reshape(b, s, LANES), bias)
            qaug_t = _fox_query_pieces(c_pieces)
            kaug_spec = pl.BlockSpec((1, s, LANES), lambda bi, ci: (bi, 0, 0))
            qaug_spec = pl.BlockSpec((1, 2, FOX_AUG_ROWS - FOX_ONE_LANE, s), lambda bi, ci: (bi, ci, 0, 0))
            o_f = _flash_t("fox", t3(q), 0, k.reshape(b, s, -1), 0, t3(v), 0, FOX_HEADS // 2, True, BF16,
                           extras=((kaug, kaug_spec), (qaug_t, qaug_spec)))
            mix_in, mix_w = [o_f.reshape(n, -1)], [w_out_odd[li].astype(BF16)]
        xf = _mix_mlp(mix_in, mix_w, xf, mlp_w_up[layer].astype(BF16), mlp_w_down[layer].astype(BF16),
                      row(ln_gain[layer, 0]), row(ln_bias[layer, 0]), row(ln_gain[layer, 1]), row(ln_bias[layer, 1]))
    return xf.reshape(b, s, d)
```

```python
import functools
import math

import numpy as np
import jax
import jax.numpy as jnp
from jax import lax
from jax.experimental import pallas as pl
from jax.experimental.pallas import tpu as pltpu

F32 = jnp.float32
BF16 = jnp.bfloat16

D_MODEL = 1024
DEPTH = 4
HEAD_DIM = 64
ROPE_DIM = HEAD_DIM // 4
ROPE_THETA = 500000.0
LN_EPS = 1e-5
DIFF_HEADS = 4
NSA_HEADS = 8
NSA_KV_HEADS = 2
NSA_HPG = NSA_HEADS // NSA_KV_HEADS
NSA_CMP_LEN = 32
NSA_CMP_STRIDE = 16
NSA_CMP_HIDDEN = 256
NSA_SEL_LEN = 64
NSA_TOPK = 8
NSA_WINDOW = 512
NSA_N_BRANCH = 3
NSA_BIG = 1e4
FOX_HEADS = D_MODEL // HEAD_DIM
D_FF = 4 * D_MODEL
ALPHA = (2 * DEPTH) ** 0.25

LANES = 128
NEG = -1e30
VMEM_LIMIT = 56 * 1024 * 1024
LOG2E = 1.4426950408889634
MASK_BIG = 30000.0
FOX_ONE_LANE = 3 * 16
FOX_AUG_ROWS = HEAD_DIM
ONES_ROWS = 16
SCORE_LOOKAHEAD = 8

EVEN_SEGS = ((0, 1024, True), (1024, 768, True), (1792, 768, False),
             (2560, 128, False), (2688, 128, False), (2816, 128, False))
EVEN_DTYPES = (BF16, BF16, BF16, BF16, BF16, F32)
ODD_SEGS = ((0, D_MODEL, False), (D_MODEL, D_MODEL, False), (2 * D_MODEL, D_MODEL, False),
            (3 * D_MODEL, 128, False))
ODD_DTYPES = (BF16, BF16, BF16, F32)


def _dot(a, b):
    return jnp.dot(a, b, preferred_element_type=F32)


def _dot_nt(a, b):
    return lax.dot_general(a, b, (((1,), (1,)), ((), ())), preferred_element_type=F32)


def _split3(x):
    p0 = x.astype(BF16)
    r1 = x - p0.astype(F32)
    p1 = r1.astype(BF16)
    p2 = (r1 - p1.astype(F32)).astype(BF16)
    return p0, p1, p2


def _rope(h, cos, sa, sb):
    return h * cos + pltpu.roll(h, LANES - ROPE_DIM // 2, 1) * sa + pltpu.roll(h, ROPE_DIM // 2, 1) * sb


def _layer_norm(y, g, b):
    mu = jnp.mean(y, axis=-1, keepdims=True)
    yc = y - mu
    var = jnp.mean(yc * yc, axis=-1, keepdims=True)
    return yc * lax.rsqrt(var + LN_EPS) * g + b


def _params(*sem):
    return pltpu.CompilerParams(dimension_semantics=sem, vmem_limit_bytes=VMEM_LIMIT)


def _proj_kernel(*refs, segs, has_rope):
    x_ref, w_ref = refs[0], refs[1]
    pos = 2
    if has_rope:
        cos_ref, sa_ref, sb_ref = refs[2:5]
        pos = 5
    out_refs = refs[pos:]
    xb = x_ref[...].astype(BF16)
    for o_ref, (c0, width, rope) in zip(out_refs, segs):
        for cc in range(0, width, 512):
            w = min(512, width - cc)
            h = _dot(xb, w_ref[:, c0 + cc:c0 + cc + w])
            if rope:
                cos, sa, sb = cos_ref[...], sa_ref[...], sb_ref[...]
                for t in range(0, w, LANES):
                    o_ref[:, cc + t:cc + t + LANES] = _rope(h[:, t:t + LANES], cos, sa, sb).astype(o_ref.dtype)
            else:
                o_ref[:, cc:cc + w] = h.astype(o_ref.dtype)


def _proj(x2d, w, tables, segs, dtypes, seq, tm=512):
    n, d = x2d.shape
    m = w.shape[1]
    ns = seq // tm
    has_rope = tables is not None
    in_specs = [pl.BlockSpec((tm, d), lambda i: (i, 0)), pl.BlockSpec((d, m), lambda i: (0, 0))]
    args = [x2d, w]
    if has_rope:
        in_specs += [pl.BlockSpec((tm, LANES), lambda i: (i % ns, 0))] * 3
        args += list(tables)
    return pl.pallas_call(
        functools.partial(_proj_kernel, segs=segs, has_rope=has_rope),
        grid=(n // tm,),
        in_specs=in_specs,
        out_specs=[pl.BlockSpec((tm, s[1]), lambda i: (i, 0)) for s in segs],
        out_shape=[jax.ShapeDtypeStruct((n, s[1]), dt) for s, dt in zip(segs, dtypes)],
        compiler_params=_params("parallel"),
        name="in_proj",
    )(*args)


def _mix_mlp_kernel(*refs, n_in, ff_chunk):
    o_refs = refs[:n_in]
    w_refs = refs[n_in:2 * n_in]
    x_ref, wu_ref, wd_ref, g1_ref, b1_ref, g2_ref, b2_ref, y_ref = refs[2 * n_in:]
    mix = _dot(o_refs[0][...], w_refs[0][...])
    for o_ref, w_ref in zip(o_refs[1:], w_refs[1:]):
        mix = mix + _dot(o_ref[...], w_ref[...])
    x = _layer_norm(ALPHA * x_ref[...] + mix, g1_ref[...], b1_ref[...])
    xb = x.astype(BF16)
    acc = None
    for c0 in range(0, wu_ref.shape[1], ff_chunk):
        h = jnp.maximum(_dot(xb, wu_ref[:, c0:c0 + ff_chunk]), 0.0)
        part = _dot((h * h).astype(BF16), wd_ref[c0:c0 + ff_chunk, :])
        acc = part if acc is None else acc + part
    y_ref[...] = _layer_norm(ALPHA * x + acc, g2_ref[...], b2_ref[...])


def _mix_mlp(os_, ws, x2d, wu, wd, g1, b1, g2, b2, tm=512, ff_chunk=1024):
    n, d = x2d.shape
    resident = lambda a: pl.BlockSpec(a.shape, lambda i: (0, 0), pipeline_mode=pl.Buffered(1))
    rowvec = pl.BlockSpec((1, d), lambda i: (0, 0))
    in_specs = [pl.BlockSpec((tm, o.shape[1]), lambda i: (i, 0)) for o in os_]
    in_specs += [resident(w) for w in ws]
    in_specs += [pl.BlockSpec((tm, d), lambda i: (i, 0)), resident(wu), resident(wd), rowvec, rowvec, rowvec, rowvec]
    return pl.pallas_call(
        functools.partial(_mix_mlp_kernel, n_in=len(os_), ff_chunk=ff_chunk),
        grid=(n // tm,),
        in_specs=in_specs,
        out_specs=pl.BlockSpec((tm, d), lambda i: (i, 0)),
        out_shape=jax.ShapeDtypeStruct((n, d), F32),
        compiler_params=_params("parallel"),
        name="mix_mlp_ln",
    )(*os_, *ws, x2d, wu, wd, g1, b1, g2, b2)


def _fox_cumsum_kernel(f_ref, bias_ref, tri_ref, scat_ref, onesrow_ref, cp_ref, kaug_ref, *, blk):
    tri = tri_ref[...]
    carry = jnp.zeros((1, LANES), F32)
    for r0 in range(0, f_ref.shape[1], blk):
        z = f_ref[0, r0:r0 + blk, :] + bias_ref[...]
        lf = jnp.minimum(z, 0.0) - jnp.log1p(jnp.exp(-jnp.abs(z)))
        p0, p1, p2 = _split3(lf)
        cs = _dot(tri, p0) + _dot(tri, p1) + _dot(tri, p2) + carry
        pieces = _split3(cs * LOG2E)
        kaug = onesrow_ref[...]
        for i, piece in enumerate(pieces):
            cp_ref[i, 0, r0:r0 + blk, :] = piece
            kaug = kaug + _dot(piece, scat_ref[i])
        kaug_ref[0, r0:r0 + blk, :] = kaug.astype(BF16)
        carry = cs[blk - 1:blk, :]


def _fox_cumsum(f3d, bias, blk=256):
    b, s, _ = f3d.shape
    blk = min(blk, s)
    tri = jnp.asarray(np.tril(np.ones((blk, blk), np.float32)), BF16)
    scat = np.zeros((3, LANES, LANES), np.float32)
    onesrow = np.zeros((1, LANES), np.float32)
    for half in (0, HEAD_DIM):
        for head in range(FOX_HEADS):
            for i in range(3):
                scat[i, head, half + 3 * head + i] = 1.0
        onesrow[0, half + FOX_ONE_LANE:half + FOX_ONE_LANE + 3] = 1.0
    return pl.pallas_call(
        functools.partial(_fox_cumsum_kernel, blk=blk),
        grid=(b,),
        in_specs=[pl.BlockSpec((1, s, LANES), lambda i: (i, 0, 0)),
                  pl.BlockSpec((1, LANES), lambda i: (0, 0)),
                  pl.BlockSpec((blk, blk), lambda i: (0, 0)),
                  pl.BlockSpec((3, LANES, LANES), lambda i: (0, 0, 0)),
                  pl.BlockSpec((1, LANES), lambda i: (0, 0))],
        out_specs=[pl.BlockSpec((3, 1, s, LANES), lambda i: (0, i, 0, 0)),
                   pl.BlockSpec((1, s, LANES), lambda i: (i, 0, 0))],
        out_shape=[jax.ShapeDtypeStruct((3, b, s, LANES), BF16),
                   jax.ShapeDtypeStruct((b, s, LANES), BF16)],
        compiler_params=_params("parallel"),
        name="fox_cumsum",
    )(f3d, bias, tri, jnp.asarray(scat, BF16), jnp.asarray(onesrow))


def _gate_lanes(gate_logits, r):
    sig = jax.nn.sigmoid(gate_logits)
    hi = sig.astype(BF16)
    lo = (sig - hi.astype(F32)).astype(BF16)
    return _dot(hi, r) + _dot(lo, r)


def _flash_t_kernel(*refs, mode, tk, qc, window, lam_init):
    it = iter(refs)
    qt_ref, k_ref, vt_ref = next(it), next(it), next(it)
    if mode in ("fox", "sel"):
        kaug_ref, qaugt_ref = next(it), next(it)
    if mode == "diff":
        lam_ref, subln_ref = next(it), next(it)
    if mode in ("sel", "win"):
        gate_ref, r_ref, prev_ref = next(it), next(it), next(it)
    eye_ref = next(it)
    o_ref, qa_ref, m_ref, acc_ref = next(it), next(it), next(it), next(it)

    seq = k_ref.shape[1]
    n_t, n_q = seq // tk, seq // qc
    vrows = acc_ref.shape[2] - ONES_ROWS

    qt = qt_ref[0]
    if mode == "fox":
        row = lax.broadcasted_iota(jnp.int32, (FOX_ONE_LANE, seq), 0)
        augs = []
        for h in (0, 1):
            own = (row - 3 * (2 * pl.program_id(1) + h)).astype(jnp.uint32) < 3
            augs.append(jnp.concatenate([jnp.where(own, -1.0, 0.0).astype(BF16), qaugt_ref[0, h]], axis=0))
    elif mode == "sel":
        augs = (qaugt_ref[0, 0], qaugt_ref[0, 1])
    else:
        augs = (jnp.zeros((HEAD_DIM, seq), BF16),) * 2
    for qi in range(n_q):
        cs = slice(qi * qc, (qi + 1) * qc)
        qa_ref[0, qi] = jnp.concatenate([qt[0:HEAD_DIM, cs], augs[0][:, cs]], axis=0)
        qa_ref[1, qi] = jnp.concatenate([augs[1][:, cs], qt[HEAD_DIM:LANES, cs]], axis=0)
    m_ref[...] = jnp.full(m_ref.shape, NEG, F32)
    acc_ref[...] = jnp.zeros(acc_ref.shape, F32)

    lo_half_k = lax.broadcasted_iota(jnp.int32, (tk, LANES), 1) < HEAD_DIM
    ones = jnp.ones((ONES_ROWS, tk), BF16)
    n_back = window // tk

    def colmax(x):
        r = x.shape[0]
        while r > 8:
            r //= 2
            x = jnp.maximum(x[0:r], x[r:2 * r])
        return jnp.max(x, axis=0, keepdims=True)

    chains = []
    for d in range(n_t):
        for h in (0, 1):
            for qi in range(n_q):
                if (d <= qi <= d + n_back) if mode == "win" else (qi >= d):
                    chains.append((d, h, qi))

    key_tiles, value_tiles = {}, {}

    def key_tile(d, h):
        if (d, h) not in key_tiles:
            k0 = d * tk
            ka = k_ref[0, k0:k0 + tk, :]
            if mode in ("fox", "sel"):
                kaug = kaug_ref[0, k0:k0 + tk, :] if mode == "fox" else kaug_ref[k0:k0 + tk, :]
                ka = jnp.where(lo_half_k, ka, kaug) if h == 0 else jnp.where(lo_half_k, kaug, ka)
            key_tiles[(d, h)] = ka
        return key_tiles[(d, h)]

    def value_tile(d, h):
        if (d, h) not in value_tiles:
            k0 = d * tk
            vt = vt_ref[0, :, k0:k0 + tk] if mode == "diff" else vt_ref[0, h * HEAD_DIM:(h + 1) * HEAD_DIM, k0:k0 + tk]
            value_tiles[(d, h)] = jnp.concatenate([vt, ones], axis=0)
        return value_tiles[(d, h)]

    def scores(d, h, qi):
        k0 = d * tk
        s = _dot(key_tile(d, h), qa_ref[h, qi])
        if qi == d or (mode == "win" and qi == d + n_back):
            rows = lax.broadcasted_iota(jnp.int32, (tk, qc), 0)
            cols = lax.broadcasted_iota(jnp.int32, (tk, qc), 1)
            dist = (qi * qc - k0) + cols - rows
            if qi == d:
                s = jnp.where(dist >= 0, s, NEG)
            else:
                s = jnp.where(dist < window, s, NEG)
        return s

    pending = [scores(*ch) for ch in chains[:SCORE_LOOKAHEAD]]
    for idx, (d, h, qi) in enumerate(chains):
        s = pending.pop(0)
        m_prev = m_ref[h, qi]
        m_new = jnp.maximum(m_prev, colmax(s))
        p = jnp.exp2((s - m_new).astype(BF16))
        alpha = jnp.exp2(m_prev - m_new)
        acc_ref[h, qi] = alpha * acc_ref[h, qi] + _dot(value_tile(d, h), p)
        m_ref[h, qi] = m_new
        if idx + SCORE_LOOKAHEAD < len(chains):
            pending.append(scores(*chains[idx + SCORE_LOOKAHEAD]))

    if mode == "diff":
        lp = lam_ref[...]
        lam = (jnp.exp(jnp.sum(lp[0:1] * lp[1:2], axis=-1, keepdims=True))
               - jnp.exp(jnp.sum(lp[2:3] * lp[3:4], axis=-1, keepdims=True)) + lam_init)
    eye = eye_ref[...]
    for qi in range(n_q):
        c0 = qi * qc
        outs = [acc_ref[h, qi, 0:vrows, :] / acc_ref[h, qi, vrows:vrows + 1, :] for h in (0, 1)]
        if mode == "diff":
            o_t = outs[0] - lam * outs[1]
            ms = jnp.mean(o_t * o_t, axis=0, keepdims=True)
            o_t = o_t * lax.rsqrt(ms + LN_EPS) * subln_ref[...] * (1.0 - lam_init)
        else:
            o_t = jnp.concatenate(outs, axis=0)
        o = _dot_nt(eye, o_t.astype(BF16))
        if mode in ("sel", "win"):
            o = prev_ref[0, c0:c0 + qc, :] + _gate_lanes(gate_ref[0, c0:c0 + qc, :], r_ref[0]) * o
        o_ref[0, c0:c0 + qc, :] = o.astype(o_ref.dtype)


def _flash_t(mode, qt_arr, q_off, k_arr, k_off, vt_arr, v_off, n_chunks, kv_per_chunk, out_dtype,
             extras=(), window=0, lam_init=0.0, tk=256, qc=256):
    b, s, _ = k_arr.shape
    tk, qc = min(tk, s), min(qc, s)
    k_map = (lambda bi, ci: (bi, 0, k_off + ci)) if kv_per_chunk else (lambda bi, ci: (bi, 0, k_off))
    v_map = (lambda bi, ci: (bi, v_off + ci, 0)) if kv_per_chunk else (lambda bi, ci: (bi, v_off, 0))
    eye = jnp.asarray(np.eye(qc, dtype=np.float32), BF16)
    in_specs = [pl.BlockSpec((1, LANES, s), lambda bi, ci: (bi, q_off + ci, 0)),
                pl.BlockSpec((1, s, LANES), k_map),
                pl.BlockSpec((1, LANES, s), v_map)]
    args = [qt_arr, k_arr, vt_arr]
    for arr, spec in extras:
        args.append(arr)
        in_specs.append(spec)
    args.append(eye)
    in_specs.append(pl.BlockSpec((qc, qc), lambda bi, ci: (0, 0)))
    acc_rows = (LANES if mode == "diff" else HEAD_DIM) + ONES_ROWS
    return pl.pallas_call(
        functools.partial(_flash_t_kernel, mode=mode, tk=tk, qc=qc, window=window, lam_init=lam_init),
        grid=(b, n_chunks),
        in_specs=in_specs,
        out_specs=pl.BlockSpec((1, s, LANES), lambda bi, ci: (bi, 0, ci)),
        out_shape=jax.ShapeDtypeStruct((b, s, n_chunks * LANES), out_dtype),
        scratch_shapes=[pltpu.VMEM((2, s // qc, LANES, qc), BF16), pltpu.VMEM((2, s // qc, 1, qc), F32),
                        pltpu.VMEM((2, s // qc, acc_rows, qc), F32)],
        compiler_params=_params("parallel", "parallel"),
        name="flash_" + mode,
    )(*args)


def _nsa_compress_kernel(tk_ref, tv_ref, w1a_ref, w1b_ref, pe_ref, b1_ref, w2_ref,
                         cos_ref, sa_ref, sb_ref, kc_ref, vc_ref):
    ncp = tk_ref.shape[1]
    for j, (t_ref, o_ref) in enumerate(((tk_ref, kc_ref), (tv_ref, vc_ref))):
        t = t_ref[0]
        u = _dot(t, w1a_ref[j])
        v = _dot(t, w1b_ref[j])
        const = _dot(pe_ref[j, 0], w1a_ref[j]) + _dot(pe_ref[j, 1], w1b_ref[j])
        pre = u + pltpu.roll(v, ncp - 1, 0) + const[0:1] + b1_ref[j]
        h = jax.nn.gelu(pre, approximate=True)
        out = _dot(h.astype(BF16), w2_ref[j])
        if j == 0:
            out = _rope(out, cos_ref[...], sa_ref[...], sb_ref[...])
        o_ref[0] = out.astype(o_ref.dtype)


def _nsa_compress(tkc, tvc, w1a, w1b, pe, b1, w2, tables_c):
    b, ncp, width = tkc.shape
    full = lambda a: pl.BlockSpec(a.shape, lambda i: (0,) * a.ndim)
    return pl.pallas_call(
        _nsa_compress_kernel,
        grid=(b,),
        in_specs=[pl.BlockSpec((1, ncp, width), lambda i: (i, 0, 0)),
                  pl.BlockSpec((1, ncp, width), lambda i: (i, 0, 0)),
                  full(w1a), full(w1b), full(pe), full(b1), full(w2)] + [full(t) for t in tables_c],
        out_specs=[pl.BlockSpec((1, ncp, LANES), lambda i: (i, 0, 0))] * 2,
        out_shape=[jax.ShapeDtypeStruct((b, ncp, LANES), BF16)] * 2,
        compiler_params=_params("parallel"),
        name="nsa_compress",
    )(tkc, tvc, w1a, w1b, pe, b1, w2, *tables_c)


def _nsa_cmp_kernel(q_ref, kc_ref, vc_ref, gate_ref, r_ref, ovt_ref, oc_ref, nsel_ref, *, tq, nsel, topk):
    qi = pl.program_id(1)
    kc = kc_ref[0]
    vc = vc_ref[0]
    ncp = kc.shape[0]
    ovt = ovt_ref[...]
    nsr = -(-nsel // 8) * 8
    lane = lax.broadcasted_iota(jnp.int32, (tq, LANES), 1)
    lo_half = lane < HEAD_DIM
    tpos_c = qi * tq + lax.broadcasted_iota(jnp.int32, (tq, ncp), 0)
    ncol = lax.broadcasted_iota(jnp.int32, (tq, ncp), 1)
    valid = (NSA_CMP_STRIDE * ncol + NSA_CMP_LEN - 1) <= tpos_c
    blk = lax.broadcasted_iota(jnp.int32, (nsr, tq), 0)
    cur = (qi * tq + lax.broadcasted_iota(jnp.int32, (nsr, tq), 1)) // NSA_SEL_LEN
    gates = jax.nn.sigmoid(gate_ref[0])
    g_hi = gates.astype(BF16)
    g_lo = (gates - g_hi.astype(F32)).astype(BF16)

    o_chunks = [None] * NSA_HPG
    gap = jnp.zeros((HEAD_DIM - nsr, tq), F32)
    logits = {}
    for g in range(NSA_KV_HEADS):
        in_half = lo_half if g == 0 else jnp.logical_not(lo_half)
        for j in range(NSA_HPG):
            qc = q_ref[0, :, j * LANES:(j + 1) * LANES]
            logits[g, j] = _dot_nt(jnp.where(in_half, qc, jnp.zeros_like(qc)), kc)
    for g in range(NSA_KV_HEADS):
        psum = jnp.zeros((tq, ncp), F32)
        for j in range(NSA_HPG):
            s = jnp.where(valid, logits[g, j], NEG)
            e = jnp.exp2(s - jnp.max(s, axis=-1, keepdims=True))
            p = jnp.where(valid, e / jnp.sum(e, axis=-1, keepdims=True), 0.0)
            psum = psum + p
            o = _dot(p.astype(BF16), vc)
            o_chunks[j] = o if g == 0 else jnp.where(lo_half, o_chunks[j], o)
        p_hi = psum.astype(BF16)
        p_lo = (psum - p_hi.astype(F32)).astype(BF16)
        imp = (_dot_nt(ovt, p_hi) + _dot_nt(ovt, p_lo))[0:nsr]
        imp = jnp.where(blk > cur, -NSA_BIG, imp)
        imp = jnp.where(blk == cur - 1, NSA_BIG, imp)
        imp = jnp.where(blk == cur, NSA_BIG, imp)
        imp = jnp.where(blk == 0, NSA_BIG, imp)
        imp = jnp.where(blk >= nsel, -3.0 * NSA_BIG, imp)
        rank = jnp.zeros((nsr, tq), jnp.int32)
        for jp in range(nsel):
            other = imp[jp:jp + 1, :]
            ahead = jnp.where(other > imp, 1, jnp.where(other == imp, jnp.where(blk > jp, 1, 0), 0))
            rank = rank + ahead
        nsel_ref[0, g] = jnp.concatenate([jnp.where(rank < topk, 0.0, 1.0), gap], axis=0).astype(nsel_ref.dtype)
    for j in range(NSA_HPG):
        r = r_ref[j * NSA_N_BRANCH]
        oc_ref[0, :, j * LANES:(j + 1) * LANES] = (_dot(g_hi, r) + _dot(g_lo, r)) * o_chunks[j]


def _nsa_cmp(q3d, q_blk, kc, vc, gate3d, r_tab, ovt, nsel, tq=256):
    b, s, _ = q3d.shape
    tq = min(tq, s)
    ncp = kc.shape[1]
    width = NSA_HPG * LANES
    full = lambda a: pl.BlockSpec(a.shape, lambda bi, qi: (0,) * a.ndim)
    return pl.pallas_call(
        functools.partial(_nsa_cmp_kernel, tq=tq, nsel=nsel, topk=min(NSA_TOPK, nsel)),
        grid=(b, s // tq),
        in_specs=[pl.BlockSpec((1, tq, width), lambda bi, qi: (bi, qi, q_blk)),
                  pl.BlockSpec((1, ncp, LANES), lambda bi, qi: (bi, 0, 0)),
                  pl.BlockSpec((1, ncp, LANES), lambda bi, qi: (bi, 0, 0)),
                  pl.BlockSpec((1, tq, LANES), lambda bi, qi: (bi, qi, 0)),
                  full(r_tab), full(ovt)],
        out_specs=[pl.BlockSpec((1, tq, width), lambda bi, qi: (bi, qi, 0)),
                   pl.BlockSpec((1, NSA_KV_HEADS, HEAD_DIM, tq), lambda bi, qi: (bi, 0, 0, qi))],
        out_shape=[jax.ShapeDtypeStruct((b, s, width), F32),
                   jax.ShapeDtypeStruct((b, NSA_KV_HEADS, HEAD_DIM, s), BF16)],
        compiler_params=_params("parallel", "arbitrary"),
        name="nsa_cmp_topk",
    )(q3d, kc, vc, gate3d, r_tab, ovt)


def _rope_tables(pos):
    half = ROPE_DIM // 2
    inv = ROPE_THETA ** (-jnp.arange(half, dtype=F32) / half)
    ang = pos.astype(F32)[:, None] * inv[None, :]
    cos, sin = jnp.cos(ang), jnp.sin(ang)
    ones = jnp.ones((pos.shape[0], HEAD_DIM - ROPE_DIM), F32)
    zeros = jnp.zeros((pos.shape[0], HEAD_DIM - ROPE_DIM), F32)
    zh = jnp.zeros_like(sin)
    c64 = jnp.concatenate([cos, cos, ones], axis=1)
    sa64 = jnp.concatenate([-sin, zh, zeros], axis=1)
    sb64 = jnp.concatenate([zh, sin, zeros], axis=1)
    return tuple(jnp.concatenate([t, t], axis=1) for t in (c64, sa64, sb64))


def _even_in_weight(w):
    o = np.cumsum([0, 512, 512, 512, 512, 128, 128, 128, 128, 128, 128, 24])
    qa, ka, va, qn, kc, vc, ksl, vsl, kw, vw, g = (np.arange(o[i], o[i + 1]) for i in range(11))
    span = lambda a: w[:, int(a[0]):int(a[-1]) + 1]
    qscale = HEAD_DIM ** -0.5 * LOG2E
    qn_heads = [span(qn[(gi * NSA_HPG + j) * HEAD_DIM:(gi * NSA_HPG + j + 1) * HEAD_DIM]) * qscale
                for j in range(NSA_HPG) for gi in range(NSA_KV_HEADS)]
    parts = [span(qa) * qscale] + qn_heads + [span(a) for a in (ka, ksl, kw, va, vsl, vw, kc, vc, g)]
    parts.append(jnp.zeros((w.shape[0], LANES - g.shape[0]), w.dtype))
    return jnp.concatenate(parts, axis=1).astype(BF16)


def _even_out_weight(w):
    heads = [w[512 + (gi * NSA_HPG + j) * HEAD_DIM:512 + (gi * NSA_HPG + j + 1) * HEAD_DIM]
             for j in range(NSA_HPG) for gi in range(NSA_KV_HEADS)]
    return w[:512].astype(BF16), jnp.concatenate(heads, axis=0).astype(BF16)


def _odd_in_weight(w):
    scale = np.ones(w.shape[1], np.float32)
    scale[:D_MODEL] = HEAD_DIM ** -0.5 * LOG2E
    wp = w * jnp.asarray(scale)[None, :]
    return jnp.pad(wp, ((0, 0), (0, LANES - FOX_HEADS))).astype(BF16)


def _compress_weights(pe, w1, b1, w2):
    eye = jnp.eye(NSA_KV_HEADS, dtype=F32)
    w1r = w1.reshape(2, 2, NSA_CMP_STRIDE, HEAD_DIM, NSA_CMP_HIDDEN)
    w1x = jnp.einsum("jardc,gh->jargdhc", w1r, eye)
    w1x = w1x.reshape(2, 2, NSA_CMP_STRIDE * LANES, NSA_KV_HEADS * NSA_CMP_HIDDEN)
    pe_r = pe.reshape(2, 2, NSA_CMP_STRIDE, 1, HEAD_DIM)
    pe_x = jnp.broadcast_to(pe_r, (2, 2, NSA_CMP_STRIDE, NSA_KV_HEADS, HEAD_DIM)).reshape(2, 2, 1, NSA_CMP_STRIDE * LANES)
    pe_x = jnp.broadcast_to(pe_x, (2, 2, 8, NSA_CMP_STRIDE * LANES))
    b1x = jnp.tile(b1, (1, NSA_KV_HEADS)).reshape(2, 1, NSA_KV_HEADS * NSA_CMP_HIDDEN)
    w2x = jnp.einsum("jcd,gh->jgchd", w2, eye).reshape(2, NSA_KV_HEADS * NSA_CMP_HIDDEN, LANES)
    return (w1x[:, 0].astype(BF16), w1x[:, 1].astype(BF16), pe_x.astype(BF16), b1x.astype(F32), w2x.astype(BF16))


def _nsa_constants(seq):
    ncp = seq // NSA_CMP_STRIDE
    nsel = seq // NSA_SEL_LEN
    n = np.arange(ncp)[:, None]
    j = np.arange(LANES)[None, :]
    cs, ce = n * NSA_CMP_STRIDE, n * NSA_CMP_STRIDE + NSA_CMP_LEN - 1
    ss = j * NSA_SEL_LEN
    ov = ((cs <= ss + NSA_SEL_LEN - 1) & (ce >= ss) & (j < nsel) & (n < ncp - 1)).astype(np.float32)
    tok = np.arange(seq)
    eneg = -MASK_BIG * (tok[:, None] // NSA_SEL_LEN == (np.arange(LANES) % HEAD_DIM)[None, :]).astype(np.float32)
    r = np.zeros((NSA_HPG * NSA_N_BRANCH, LANES, LANES), np.float32)
    for jc in range(NSA_HPG):
        for br in range(NSA_N_BRANCH):
            for ln in range(LANES):
                head = (ln // HEAD_DIM) * NSA_HPG + jc
                r[jc * NSA_N_BRANCH + br, head * NSA_N_BRANCH + br, ln] = 1.0
    return jnp.asarray(ov.T.copy(), BF16), jnp.asarray(eneg, BF16), jnp.asarray(r, BF16), nsel


def _fox_query_pieces(c_pieces):
    pieces = jnp.transpose(c_pieces[..., :FOX_HEADS], (1, 3, 0, 2))
    return jnp.pad(pieces, ((0, 0), (0, 0), (0, FOX_AUG_ROWS - FOX_ONE_LANE - 3), (0, 0)))


def kernel(x, ln_gain, ln_bias, mlp_w_up, mlp_w_down, w_in_even, w_out_even, diff_lambda, diff_subln,
           nsa_pe, nsa_cmp_w1, nsa_cmp_b1, nsa_cmp_w2, w_in_odd, fox_f_bias, w_out_odd):
    b, s, d = x.shape
    n = b * s
    ncp = s // NSA_CMP_STRIDE
    xf = x.reshape(n, d)
    tables = _rope_tables(jnp.arange(s))
    tables_c = _rope_tables(jnp.arange(ncp) * NSA_CMP_STRIDE + NSA_CMP_LEN - 1)
    ovt, eneg, r_tab, nsel = _nsa_constants(s)
    row = lambda a: a.reshape(1, -1).astype(F32)
    t3 = lambda a: jnp.transpose(a.reshape(b, s, -1), (0, 2, 1))
    full2 = lambda a: pl.BlockSpec(a.shape, lambda bi, ci: (0, 0))

    for layer in range(DEPTH):
        li = layer // 2
        if layer % 2 == 0:
            q, k, vv, kci, vci, gate = _proj(xf, _even_in_weight(w_in_even[li]), tables, EVEN_SEGS, EVEN_DTYPES, s)
            q3, qt3, k3, vt3, gate3 = q.reshape(b, s, -1), t3(q), k.reshape(b, s, -1), t3(vv), gate.reshape(b, s, LANES)
            lam_init = 0.8 - 0.6 * math.exp(-0.3 * layer)
            lam_p = diff_lambda[li].astype(F32)
            subln = jnp.broadcast_to(diff_subln[li].astype(F32)[:, None], (LANES, min(256, s)))
            o_a = _flash_t("diff", qt3, 0, k3, 0, vt3, 0, DIFF_HEADS, True, BF16,
                           extras=((lam_p, full2(lam_p)), (subln, full2(subln))), lam_init=lam_init)
            w1a, w1b, pe_x, b1x, w2x = _compress_weights(nsa_pe[li], nsa_cmp_w1[li], nsa_cmp_b1[li], nsa_cmp_w2[li])
            kc, vc = _nsa_compress(kci.reshape(b, ncp, -1), vci.reshape(b, ncp, -1), w1a, w1b, pe_x, b1x, w2x, tables_c)
            o_c, notsel_t = _nsa_cmp(q3, 1, kc, vc, gate3, r_tab, ovt, nsel)
            gate_spec = pl.BlockSpec((1, s, LANES), lambda bi, ci: (bi, 0, 0))
            prev_spec = pl.BlockSpec((1, s, LANES), lambda bi, ci: (bi, 0, ci))
            r_spec = lambda br: pl.BlockSpec((1, LANES, LANES), lambda bi, ci: (ci * NSA_N_BRANCH + br, 0, 0))
            nsel_spec = pl.BlockSpec((1, NSA_KV_HEADS, HEAD_DIM, s), lambda bi, ci: (bi, 0, 0, 0))
            o_cs = _flash_t("sel", qt3, 4, k3, 4, vt3, 4, NSA_HPG, False, F32,
                            extras=((eneg, full2(eneg)), (notsel_t, nsel_spec), (gate3, gate_spec), (r_tab, r_spec(1)),
                                    (o_c, prev_spec)))
            o_b = _flash_t("win", qt3, 4, k3, 5, vt3, 5, NSA_HPG, False, BF16,
                           extras=((gate3, gate_spec), (r_tab, r_spec(2)), (o_cs, prev_spec)), window=NSA_WINDOW)
            wo_a, wo_b = _even_out_weight(w_out_even[li])
            mix_in, mix_w = [o_a.reshape(n, -1), o_b.reshape(n, -1)], [wo_a, wo_b]
        else:
            q, k, v, fl = _proj(xf, _odd_in_weight(w_in_odd[li]), None, ODD_SEGS, ODD_DTYPES, s)
            bias = jnp.pad(fox_f_bias[li].astype(F32), (0, LANES - FOX_HEADS)).reshape(1, LANES)
            c_pieces, kaug = _fox_cumsum(fl.reshape(b, s, LANES), bias)
            qaug_t = _fox_query_pieces(c_pieces)
            kaug_spec = pl.BlockSpec((1, s, LANES), lambda bi, ci: (bi, 0, 0))
            qaug_spec = pl.BlockSpec((1, 2, FOX_AUG_ROWS - FOX_ONE_LANE, s), lambda bi, ci: (bi, ci, 0, 0))
            o_f = _flash_t("fox", t3(q), 0, k.reshape(b, s, -1), 0, t3(v), 0, FOX_HEADS // 2, True, BF16,
                           extras=((kaug, kaug_spec), (qaug_t, qaug_spec)))
            mix_in, mix_w = [o_f.reshape(n, -1)], [w_out_odd[li].astype(BF16)]
        xf = _mix_mlp(mix_in, mix_w, xf, mlp_w_up[layer].astype(BF16), mlp_w_down[layer].astype(BF16),
                      row(ln_gain[layer, 0]), row(ln_bias[layer, 0]), row(ln_gain[layer, 1]), row(ln_bias[layer, 1]))
    return xf.reshape(b, s, d)
```
